```python
import jax, jax.numpy as jnp
from jax import lax
import numpy as np

D_MODEL = 1024
BATCH = 32
SEQ = 2048
DEPTH = 1

HGRN_HEADS = 8
HGRN_DK = 128
HGRN_DV = D_MODEL // HGRN_HEADS
HGRN_F = HGRN_HEADS * HGRN_DK
HGRN_V = HGRN_HEADS * HGRN_DV
CHUNK = 64
ATT_HEADS = 16
ATT_KV_HEADS = 2
ATT_HD = 64
ATT_GROUP = ATT_HEADS // ATT_KV_HEADS
WINDOW = 128
ROPE_DIM = ATT_HD // 4
ROPE_THETA = 500000.0
D_FF = 2816
CONV_W = 3
EPS = 1e-6
NEG_INF = -1e30
COL_SIZES = (HGRN_F, HGRN_F, HGRN_V, HGRN_V, ATT_HEADS * ATT_HD, ATT_KV_HEADS * ATT_HD, ATT_KV_HEADS * ATT_HD, D_MODEL, D_MODEL)
D_IN = HGRN_F + HGRN_F + HGRN_V + HGRN_V + ATT_HEADS * ATT_HD + 2 * ATT_KV_HEADS * ATT_HD + 2 * D_MODEL

kernel_name = 'hybrid_hgrn2_swa_convffn'


def rmsnorm(x, g):
    xf = x.astype(jnp.float32)
    y = xf * lax.rsqrt(jnp.mean(xf * xf, axis=-1, keepdims=True) + EPS)
    return (y * g.astype(jnp.float32)).astype(x.dtype)


def split_cols(z):
    idx = []
    acc = 0
    for s in COL_SIZES[:-1]:
        acc += s
        idx.append(acc)
    return jnp.split(z, idx, axis=-1)


def partial_rope(x, pos):
    half = ROPE_DIM // 2
    inv = ROPE_THETA ** (-2.0 * jnp.arange(half, dtype=jnp.float32) / ROPE_DIM)
    ang = pos.astype(jnp.float32)[..., None] * inv
    cos = jnp.cos(ang)[:, :, None, :]
    sin = jnp.sin(ang)[:, :, None, :]
    xr = x[..., :ROPE_DIM].astype(jnp.float32)
    x1, x2 = xr[..., :half], xr[..., half:]
    rot = jnp.concatenate([x1 * cos - x2 * sin, x2 * cos + x1 * sin], axis=-1).astype(x.dtype)
    return jnp.concatenate([rot, x[..., ROPE_DIM:]], axis=-1)


def hgrn2_chunkwise(q, fz, i, lb):
    B, S = q.shape[0], q.shape[1]
    n = S // CHUNK
    qf = jax.nn.silu(q.astype(jnp.float32))
    f = lb + (1.0 - lb) * jax.nn.sigmoid(fz.astype(jnp.float32))
    logf = jnp.log(f)
    k = 1.0 - f

    def chunks(t, d):
        return t.reshape(B, n, CHUNK, HGRN_HEADS, d).transpose(0, 3, 1, 2, 4)

    qc = chunks(qf, HGRN_DK)
    kc = chunks(k, HGRN_DK)
    vc = chunks(i.astype(jnp.float32), HGRN_DV)
    b = jnp.cumsum(chunks(logf, HGRN_DK), axis=3)
    bref = b[:, :, :, CHUNK // 2:CHUNK // 2 + 1]
    q_in = qc * jnp.exp(b - bref)
    k_in = kc * jnp.exp(bref - b)
    causal = jnp.tril(jnp.ones((CHUNK, CHUNK), dtype=bool))
    a = jnp.where(causal, jnp.einsum('bhncd,bhnsd->bhncs', q_in, k_in), 0.0)
    o_intra = jnp.einsum('bhncs,bhnse->bhnce', a, vc)
    blast = b[:, :, :, -1:]
    q_out = qc * jnp.exp(b)
    k_st = kc * jnp.exp(blast - b)
    dec = jnp.exp(blast[:, :, :, 0])

    def step(state, xs):
        qo, ks, vv, d = xs
        o = jnp.einsum('bhcd,bhde->bhce', qo, state)
        state = state * d[..., None] + jnp.einsum('bhcd,bhce->bhde', ks, vv)
        return state, o

    s0 = jnp.zeros((B, HGRN_HEADS, HGRN_DK, HGRN_DV), jnp.float32)
    mv = lambda t: jnp.moveaxis(t, 2, 0)
    _, o_inter = lax.scan(step, s0, (mv(q_out), mv(k_st), mv(vc), mv(dec)))
    o = o_intra + jnp.moveaxis(o_inter, 0, 2)
    return o.transpose(0, 2, 3, 1, 4).reshape(B, S, HGRN_HEADS, HGRN_DV)


def sliding_window_gqa(q, k, v, sinks, pos):
    B, S = q.shape[0], q.shape[1]
    nb = S // WINDOW
    q = partial_rope(q, pos)
    k = partial_rope(k, pos)
    qb = q.reshape(B, nb, WINDOW, ATT_KV_HEADS, ATT_GROUP, ATT_HD)

    def band_keys(t):
        tp = jnp.pad(t, ((0, 0), (WINDOW, 0), (0, 0), (0, 0))).reshape(B, nb + 1, WINDOW, ATT_KV_HEADS, ATT_HD)
        return jnp.concatenate([tp[:, :-1], tp[:, 1:]], axis=2)

    kb = band_keys(k)
    vb = band_keys(v)
    s = jnp.einsum('bnqkgd,bnmkd->bnkgqm', qb, kb).astype(jnp.float32) * (ATT_HD ** -0.5)
    qi = jnp.arange(WINDOW)[:, None]
    mi = jnp.arange(2 * WINDOW)[None, :]
    band = (mi > qi) & (mi <= qi + WINDOW)
    blk = jnp.arange(nb)[:, None, None]
    mask = band[None] & ((blk > 0) | (mi >= WINDOW)[None])
    s = jnp.where(mask[None, :, None, None], s, NEG_INF)
    sink = sinks.astype(jnp.float32).reshape(1, 1, ATT_KV_HEADS, ATT_GROUP, 1, 1)
    m = jnp.maximum(jnp.max(s, axis=-1, keepdims=True), sink)
    p = jnp.exp(s - m)
    den = jnp.sum(p, axis=-1, keepdims=True) + jnp.exp(sink - m)
    p = (p / den).astype(v.dtype)
    o = jnp.einsum('bnkgqm,bnmkd->bnqkgd', p, vb)
    return o.reshape(B, S, ATT_HEADS * ATT_HD)


def causal_dwconv(x, w, bias):
    y = lax.conv_general_dilated(x, w.astype(x.dtype)[:, None, :], window_strides=(1,), padding=[(CONV_W - 1, 0)], dimension_numbers=('NWC', 'WIO', 'NWC'), feature_group_count=x.shape[-1])
    return y + bias.astype(x.dtype)


def setup_inputs(seed: int = 0) -> dict:
    key = jax.random.key(seed)
    ks = jax.random.split(key, 18)
    nrm = lambda k, shape, fan_in: jax.random.normal(k, shape, jnp.float32) * (fan_in ** -0.5)
    x = jax.random.normal(ks[0], (BATCH, SEQ, D_MODEL), jnp.float32)
    offs = jax.random.randint(ks[1], (BATCH, 1), 0, 4096, dtype=jnp.int32)
    positions = (jnp.arange(SEQ, dtype=jnp.int32)[None, :] + offs).astype(jnp.int32)
    return {
        'x': x,
        'positions': positions,
        'norm1_g': 1.0 + 0.02 * jax.random.normal(ks[2], (DEPTH, D_MODEL), jnp.float32),
        'w_in': nrm(ks[3], (DEPTH, D_MODEL, D_IN), D_MODEL),
        'lb_logits': 0.1 * jax.random.normal(ks[4], (DEPTH + 1, HGRN_F), jnp.float32),
        'hgrn_norm_g': 1.0 + 0.02 * jax.random.normal(ks[5], (DEPTH, HGRN_DV), jnp.float32),
        'w_a': nrm(ks[6], (DEPTH, HGRN_V, D_MODEL), HGRN_V),
        'attn_sinks': 0.5 * jax.random.normal(ks[7], (DEPTH, ATT_HEADS), jnp.float32),
        'w_b': nrm(ks[8], (DEPTH, ATT_HEADS * ATT_HD, D_MODEL), ATT_HEADS * ATT_HD),
        'w_out': nrm(ks[9], (DEPTH, D_MODEL, D_MODEL), D_MODEL),
        'norm2_g': 1.0 + 0.02 * jax.random.normal(ks[10], (DEPTH, D_MODEL), jnp.float32),
        'w_ffn_in': nrm(ks[11], (DEPTH, D_MODEL, 2 * D_FF), D_MODEL),
        'conv_w': nrm(ks[12], (DEPTH, CONV_W, D_FF), CONV_W),
        'conv_b': 0.02 * jax.random.normal(ks[13], (DEPTH, D_FF), jnp.float32),
        'w_down': nrm(ks[14], (DEPTH, D_FF, D_MODEL), D_FF),
        'final_g': 1.0 + 0.02 * jax.random.normal(ks[15], (D_MODEL,), jnp.float32),
    }


def reference(x, positions, norm1_g, w_in, lb_logits, hgrn_norm_g, w_a, attn_sinks, w_b, w_out, norm2_g, w_ffn_in, conv_w, conv_b, w_down, final_g):
    B, S = x.shape[0], x.shape[1]
    lb_all = jnp.cumsum(jax.nn.softmax(lb_logits.astype(jnp.float32), axis=0), axis=0)
    h = x
    for l in range(DEPTH):
        u = rmsnorm(h, norm1_g[l])
        z = u @ w_in[l]
        hq, hf, hi, hg, aq, ak, av, ga, gb = split_cols(z)
        o_a = hgrn2_chunkwise(hq, hf, hi, lb_all[l])
        o_a = (rmsnorm(o_a, hgrn_norm_g[l]).reshape(B, S, HGRN_V) * jax.nn.silu(hg.astype(jnp.float32))).astype(x.dtype)
        o_b = sliding_window_gqa(aq.reshape(B, S, ATT_HEADS, ATT_HD), ak.reshape(B, S, ATT_KV_HEADS, ATT_HD), av.reshape(B, S, ATT_KV_HEADS, ATT_HD), attn_sinks[l], positions)
        merged = jax.nn.sigmoid(ga) * (o_a @ w_a[l]) + jax.nn.sigmoid(gb) * (o_b @ w_b[l])
        h = h + merged @ w_out[l]
        u = rmsnorm(h, norm2_g[l])
        gu = u @ w_ffn_in[l]
        g, up = gu[..., :D_FF], gu[..., D_FF:]
        a = causal_dwconv(g, conv_w[l], conv_b[l])
        h = h + (jax.nn.silu(a) * up) @ w_down[l]
    return rmsnorm(h, final_g)
```

```python
import functools

import jax
import jax.numpy as jnp
from jax import lax
from jax.experimental import pallas as pl
from jax.experimental.pallas import tpu as pltpu

D_MODEL = 1024
HGRN_HEADS = 8
HGRN_DK = 128
HGRN_DV = 128
HGRN_F = HGRN_HEADS * HGRN_DK
HGRN_V = HGRN_HEADS * HGRN_DV
CHUNK = 64
ATT_HEADS = 16
ATT_KV_HEADS = 2
ATT_HD = 64
ATT_GROUP = ATT_HEADS // ATT_KV_HEADS
WINDOW = 128
ROPE_DIM = ATT_HD // 4
ROPE_HALF = ROPE_DIM // 2
ROPE_THETA = 500000.0
D_FF = 2816
CONV_W = 3
EPS = 1e-6
NEG_INF = -1e30

OFF_HQ = 0
OFF_HF = OFF_HQ + HGRN_F
OFF_HI = OFF_HF + HGRN_F
OFF_HG = OFF_HI + HGRN_V
OFF_AQ = OFF_HG + HGRN_V
OFF_AK = OFF_AQ + ATT_HEADS * ATT_HD
OFF_AV = OFF_AK + ATT_KV_HEADS * ATT_HD
OFF_GA = OFF_AV + ATT_KV_HEADS * ATT_HD
OFF_GB = OFF_GA + D_MODEL
D_IN = OFF_GB + D_MODEL

LANES = 128
SUBLANES = 8
VMEM_LIMIT_BYTES = 56 * 1024 * 1024

MIX_TS = 256
FFN_TS = 256
FFN_COL_CHUNKS = ((0, 768), (768, 1536), (1536, 2304), (2304, 2816))

F32 = jnp.float32
BF16 = jnp.bfloat16
_NT = (((1,), (1,)), ((), ()))
_TN = (((0,), (0,)), ((), ()))


def _rmsnorm(x, g):
    return x * lax.rsqrt(jnp.mean(x * x, axis=-1, keepdims=True) + EPS) * g


def _silu(x):
    return x * jax.nn.sigmoid(x)


def _mixer_kernel(sinks_ref, x_ref, pos_ref, inv_ref, g1_ref, lbl_ref, hg_ref, win_ref, wa_ref, wb_ref,
                  wout_ref, h_ref, st_ref, kbuf, vbuf, qf_s, k_s, lf_s, v_s, oa_s, q_s, ob_s, m_s, *, ts):
    t = pl.program_id(1)

    @pl.when(t == 0)
    def _():
        st_ref[...] = jnp.zeros_like(st_ref)
        kbuf[0:WINDOW, :] = jnp.zeros((WINDOW, kbuf.shape[1]), BF16)
        vbuf[0:WINDOW, :] = jnp.zeros((WINDOW, vbuf.shape[1]), BF16)

    x = x_ref[0]
    u = _rmsnorm(x, g1_ref[...]).astype(BF16)

    def proj(lo, hi):
        return jnp.dot(u, win_ref[:, lo:hi], preferred_element_type=F32)

    zq = proj(OFF_HQ, OFF_HF)
    qf_s[...] = _silu(zq)
    lbl = lbl_ref[...]
    le = jnp.exp(lbl - jnp.max(lbl, axis=0, keepdims=True))
    lb = le[0:1] / (le[0:1] + le[1:2])
    f = lb + (1.0 - lb) * jax.nn.sigmoid(proj(OFF_HF, OFF_HI))
    k_s[...] = 1.0 - f
    lf = jnp.log(f)
    lf_hi = lf.astype(BF16)
    r1 = lf - lf_hi.astype(F32)
    lf_mid = r1.astype(BF16)
    lf_lo = (r1 - lf_mid.astype(F32)).astype(BF16)
    lf_s[0] = lf_hi
    lf_s[1] = lf_mid
    lf_s[2] = lf_lo
    v_s[...] = proj(OFF_HI, OFF_HG).astype(BF16)

    row = lax.broadcasted_iota(jnp.int32, (CHUNK, 3 * CHUNK), 0)
    col = lax.broadcasted_iota(jnp.int32, (CHUNK, 3 * CHUNK), 1)
    tri3 = (row >= (col % CHUNK)).astype(BF16)
    causal = (lax.broadcasted_iota(jnp.int32, (CHUNK, CHUNK), 0)
              >= lax.broadcasted_iota(jnp.int32, (CHUNK, CHUNK), 1))
    hg = hg_ref[...]

    def chunk_body(c, carry):
        rows = pl.ds(pl.multiple_of(c * CHUNK, CHUNK), CHUNK)
        l3 = jnp.concatenate([lf_s[0, rows, :], lf_s[1, rows, :], lf_s[2, rows, :]], axis=0)
        b = jnp.dot(tri3, l3, preferred_element_type=F32)
        bref = b[CHUNK // 2:CHUNK // 2 + 1]
        blast = b[CHUNK - 1:CHUNK]
        q_in = qf_s[rows, :] * jnp.exp(b - bref)
        k_in = k_s[rows, :] * jnp.exp(bref - b)
        q_out = (q_in * jnp.exp(bref)).astype(BF16)
        k_st = (k_in * jnp.exp(blast - bref)).astype(BF16)
        q_in = q_in.astype(BF16)
        k_in = k_in.astype(BF16)
        dec = jnp.exp(blast)
        vv = v_s[rows, :]
        for h in range(HGRN_HEADS):
            sl = slice(h * HGRN_DK, (h + 1) * HGRN_DK)
            a = lax.dot_general(q_in[:, sl], k_in[:, sl], _NT, preferred_element_type=F32)
            a = jnp.where(causal, a, 0.0).astype(BF16)
            st = st_ref[h]
            o = (jnp.dot(a, vv[:, sl], preferred_element_type=F32)
                 + lax.dot_general(q_out[:, sl], st.astype(BF16), _NT, preferred_element_type=F32))
            st_ref[h] = st * dec[:, sl] + lax.dot_general(vv[:, sl], k_st[:, sl], _TN,
                                                          preferred_element_type=F32)
            oa_s[rows, sl] = _rmsnorm(o, hg)
        return carry

    lax.fori_loop(0, ts // CHUNK, chunk_body, 0)

    oa = (oa_s[...] * _silu(proj(OFF_HG, OFF_AQ))).astype(BF16)
    ya = jnp.dot(oa, wa_ref[...], preferred_element_type=F32)
    m_s[...] = jax.nn.sigmoid(proj(OFF_GA, OFF_GB)) * ya

    pos = pos_ref[0].astype(F32)
    inv = inv_ref[...]
    ang = jnp.concatenate([inv * pos[:, j * LANES:(j + 1) * LANES] for j in range(ts // LANES)], axis=1)
    c8 = jnp.cos(ang)
    s8 = jnp.sin(ang)
    one_pad = jnp.ones((ATT_HD - ROPE_DIM, ts), F32)
    zero_pad = jnp.zeros((ATT_HD - ROPE_DIM, ts), F32)
    z8 = jnp.zeros((ROPE_HALF, ts), F32)
    reps = LANES // ATT_HD
    cos_t = jnp.concatenate([c8, c8, one_pad] * reps, axis=0).T
    sin_lo_t = jnp.concatenate([z8, s8, zero_pad] * reps, axis=0).T
    sin_hi_t = jnp.concatenate([-s8, z8, zero_pad] * reps, axis=0).T

    def rope(xb):
        return (xb * cos_t + pltpu.roll(xb, ROPE_HALF, 1) * sin_lo_t
                + pltpu.roll(xb, LANES - ROPE_HALF, 1) * sin_hi_t)

    zaq = proj(OFF_AQ, OFF_AK)
    scale = ATT_HD ** -0.5
    for i in range(ATT_HEADS * ATT_HD // LANES):
        sl = slice(i * LANES, (i + 1) * LANES)
        q_s[:, sl] = (rope(zaq[:, sl]) * scale).astype(BF16)

    zkv = proj(OFF_AK, OFF_GA)
    lane = lax.broadcasted_iota(jnp.int32, (ts, LANES), 1)
    lo_half = lane < ATT_HD

    def spread(z):
        zr = pltpu.roll(z, ATT_HD, 1)
        return jnp.concatenate([jnp.where(lo_half, z, 0.0), jnp.where(lo_half, 0.0, zr),
                                jnp.where(lo_half, zr, 0.0), jnp.where(lo_half, 0.0, z)], axis=1).astype(BF16)

    kbuf[WINDOW:WINDOW + ts, :] = spread(rope(zkv[:, 0:LANES]))
    vbuf[WINDOW:WINDOW + ts, :] = spread(zkv[:, LANES:2 * LANES])

    qi = lax.broadcasted_iota(jnp.int32, (WINDOW, 2 * WINDOW), 0)
    mi = lax.broadcasted_iota(jnp.int32, (WINDOW, 2 * WINDOW), 1)
    band = (mi > qi) & (mi <= qi + WINDOW)
    band_first = band & ((mi >= WINDOW) | (t > 0))
    lane_w = lax.broadcasted_iota(jnp.int32, (WINDOW, LANES), 1)
    pairs = ATT_GROUP // 2

    for qb in range(ts // WINDOW):
        qrows = slice(qb * WINDOW, (qb + 1) * WINDOW)
        krows = slice(qb * WINDOW, qb * WINDOW + 2 * WINDOW)
        mask = band_first if qb == 0 else band
        for kv in range(ATT_KV_HEADS):
            c0 = kv * 2 * LANES
            k2 = jnp.concatenate([kbuf[krows, c0:c0 + LANES], kbuf[krows, c0 + LANES:c0 + 2 * LANES]], axis=0)
            v2 = jnp.concatenate([vbuf[krows, c0:c0 + LANES], vbuf[krows, c0 + LANES:c0 + 2 * LANES]], axis=0)
            qs = jnp.concatenate(
                [q_s[qrows, (kv * pairs + i) * LANES:(kv * pairs + i + 1) * LANES] for i in range(pairs)], axis=0)
            s = lax.dot_general(qs, k2, _NT, preferred_element_type=F32)
            p_rows = []
            rden = []
            for i in range(pairs):
                p_cols = []
                for e in range(2):
                    blk = s[i * WINDOW:(i + 1) * WINDOW, e * 2 * WINDOW:(e + 1) * 2 * WINDOW]
                    blk = jnp.where(mask, blk, NEG_INF)
                    sink = sinks_ref[kv * ATT_GROUP + 2 * i + e]
                    m = jnp.maximum(jnp.max(blk, axis=-1, keepdims=True), sink)
                    p = jnp.exp(blk - m)
                    den = jnp.sum(p, axis=-1, keepdims=True) + jnp.exp(sink - m)
                    rden.append(1.0 / den)
                    p_cols.append(p.astype(BF16))
                p_rows.append(jnp.concatenate(p_cols, axis=1))
            p = jnp.concatenate(p_rows, axis=0)
            o = jnp.dot(p, v2, preferred_element_type=F32)
            for i in range(pairs):
                r = jnp.where(lane_w < ATT_HD, rden[2 * i], rden[2 * i + 1])
                ob_s[qrows, (kv * pairs + i) * LANES:(kv * pairs + i + 1) * LANES] = (
                    o[i * WINDOW:(i + 1) * WINDOW] * r).astype(BF16)

    kbuf[0:WINDOW, :] = kbuf[ts:ts + WINDOW, :]
    vbuf[0:WINDOW, :] = vbuf[ts:ts + WINDOW, :]

    yb = jnp.dot(ob_s[...], wb_ref[...], preferred_element_type=F32)
    merged = (m_s[...] + jax.nn.sigmoid(proj(OFF_GB, D_IN)) * yb).astype(BF16)
    h_ref[0] = x + jnp.dot(merged, wout_ref[...], preferred_element_type=F32)


def _ffn_kernel(h_ref, g2_ref, win_ref, cw_ref, cb_ref, wd_ref, fg_ref, o_ref, gbuf, *, ts):
    t = pl.program_id(1)

    @pl.when(t == 0)
    def _():
        gbuf[0:SUBLANES, :] = jnp.zeros((SUBLANES, D_FF), F32)

    h = h_ref[0]
    u = _rmsnorm(h, g2_ref[...]).astype(BF16)
    y = None
    for lo, hi in FFN_COL_CHUNKS:
        g = jnp.dot(u, win_ref[:, lo:hi], preferred_element_type=F32)
        up = jnp.dot(u, win_ref[:, D_FF + lo:D_FF + hi], preferred_element_type=F32)
        gbuf[SUBLANES:SUBLANES + ts, lo:hi] = g
        a = (cw_ref[2:3, lo:hi] * g
             + cw_ref[1:2, lo:hi] * gbuf[SUBLANES - 1:SUBLANES - 1 + ts, lo:hi]
             + cw_ref[0:1, lo:hi] * gbuf[SUBLANES - 2:SUBLANES - 2 + ts, lo:hi]
             + cb_ref[:, lo:hi])
        act = (_silu(a) * up).astype(BF16)
        part = jnp.dot(act, wd_ref[lo:hi, :], preferred_element_type=F32)
        y = part if y is None else y + part
    gbuf[0:SUBLANES, :] = gbuf[ts:ts + SUBLANES, :]
    o_ref[0] = _rmsnorm(h + y, fg_ref[...])


def _const_spec(shape):
    return pl.BlockSpec(shape, lambda b, t: (0,) * len(shape), pipeline_mode=pl.Buffered(1))


def _mixer_call(x, pos3, inv_tab, g1, lbl, hg, w_in, w_a, w_b, w_out, sinks):
    B, S, D = x.shape
    ts = MIX_TS
    assert S % ts == 0 and ts % WINDOW == 0 and ts % CHUNK == 0 and ts % LANES == 0
    tile = pl.BlockSpec((1, ts, D), lambda b, t: (b, t, 0))
    return pl.pallas_call(
        functools.partial(_mixer_kernel, ts=ts),
        grid=(B, S // ts),
        in_specs=[
            pl.BlockSpec(memory_space=pltpu.SMEM),
            tile,
            pl.BlockSpec((1, 1, ts), lambda b, t: (b, 0, t)),
            _const_spec(inv_tab.shape),
            _const_spec(g1.shape),
            _const_spec(lbl.shape),
            _const_spec(hg.shape),
            _const_spec(w_in.shape),
            _const_spec(w_a.shape),
            _const_spec(w_b.shape),
            _const_spec(w_out.shape),
        ],
        out_specs=tile,
        out_shape=jax.ShapeDtypeStruct((B, S, D), F32),
        scratch_shapes=[
            pltpu.VMEM((HGRN_HEADS, HGRN_DV, HGRN_DK), F32),
            pltpu.VMEM((WINDOW + ts, 4 * LANES), BF16),
            pltpu.VMEM((WINDOW + ts, 4 * LANES), BF16),
            pltpu.VMEM((ts, HGRN_F), F32),
            pltpu.VMEM((ts, HGRN_F), F32),
            pltpu.VMEM((3, ts, HGRN_F), BF16),
            pltpu.VMEM((ts, HGRN_V), BF16),
            pltpu.VMEM((ts, HGRN_V), F32),
            pltpu.VMEM((ts, ATT_HEADS * ATT_HD), BF16),
            pltpu.VMEM((ts, ATT_HEADS * ATT_HD), BF16),
            pltpu.VMEM((ts, D_MODEL), F32),
        ],
        compiler_params=pltpu.CompilerParams(
            dimension_semantics=("parallel", "arbitrary"), vmem_limit_bytes=VMEM_LIMIT_BYTES),
        name="token_mixer",
    )(sinks, x, pos3, inv_tab, g1, lbl, hg, w_in, w_a, w_b, w_out)


def _ffn_call(h, g2, w_ffn_in, conv_w, conv_b, w_down, fg):
    B, S, D = h.shape
    ts = FFN_TS
    assert S % ts == 0 and ts % SUBLANES == 0
    tile = pl.BlockSpec((1, ts, D), lambda b, t: (b, t, 0))
    return pl.pallas_call(
        functools.partial(_ffn_kernel, ts=ts),
        grid=(B, S // ts),
        in_specs=[
            tile,
            _const_spec(g2.shape),
            _const_spec(w_ffn_in.shape),
            _const_spec(conv_w.shape),
            _const_spec(conv_b.shape),
            _const_spec(w_down.shape),
            _const_spec(fg.shape),
        ],
        out_specs=tile,
        out_shape=jax.ShapeDtypeStruct((B, S, D), F32),
        scratch_shapes=[pltpu.VMEM((SUBLANES + ts, D_FF), F32)],
        compiler_params=pltpu.CompilerParams(
            dimension_semantics=("parallel", "arbitrary"), vmem_limit_bytes=VMEM_LIMIT_BYTES),
        name="channel_mixer",
    )(h, g2, w_ffn_in, conv_w, conv_b, w_down, fg)


def kernel(x, positions, norm1_g, w_in, lb_logits, hgrn_norm_g, w_a, attn_sinks, w_b, w_out, norm2_g, w_ffn_in,
           conv_w, conv_b, w_down, final_g):
    B, S, D = x.shape
    assert norm1_g.shape[0] == 1, "single-layer stack"
    pos3 = positions.reshape(B, 1, S)
    inv = ROPE_THETA ** (-2.0 * jnp.arange(ROPE_HALF, dtype=F32) / ROPE_DIM)
    inv_tab = jnp.broadcast_to(inv[:, None], (ROPE_HALF, LANES))
    h = _mixer_call(
        x, pos3, inv_tab, norm1_g[0][None, :], lb_logits.astype(F32), hgrn_norm_g[0][None, :],
        w_in[0].astype(BF16), w_a[0].astype(BF16), w_b[0].astype(BF16), w_out[0].astype(BF16), attn_sinks[0])
    return _ffn_call(
        h, norm2_g[0][None, :], w_ffn_in[0].astype(BF16), conv_w[0], conv_b[0][None, :],
        w_down[0].astype(BF16), final_g[None, :])
```

```python
import functools

import jax
import jax.numpy as jnp
from jax import lax
from jax.experimental import pallas as pl
from jax.experimental.pallas import tpu as pltpu

D_MODEL = 1024
HGRN_HEADS = 8
HGRN_DK = 128
HGRN_DV = 128
HGRN_F = HGRN_HEADS * HGRN_DK
HGRN_V = HGRN_HEADS * HGRN_DV
CHUNK = 64
ATT_HEADS = 16
ATT_KV_HEADS = 2
ATT_HD = 64
ATT_GROUP = ATT_HEADS // ATT_KV_HEADS
WINDOW = 128
ROPE_DIM = ATT_HD // 4
ROPE_HALF = ROPE_DIM // 2
ROPE_THETA = 500000.0
D_FF = 2816
CONV_W = 3
EPS = 1e-6
NEG_INF = -1e30

OFF_HQ = 0
OFF_HF = OFF_HQ + HGRN_F
OFF_HI = OFF_HF + HGRN_F
OFF_HG = OFF_HI + HGRN_V
OFF_AQ = OFF_HG + HGRN_V
OFF_AK = OFF_AQ + ATT_HEADS * ATT_HD
OFF_AV = OFF_AK + ATT_KV_HEADS * ATT_HD
OFF_GA = OFF_AV + ATT_KV_HEADS * ATT_HD
OFF_GB = OFF_GA + D_MODEL
D_IN = OFF_GB + D_MODEL

LANES = 128
SUBLANES = 8
VMEM_LIMIT_BYTES = 58 * 1024 * 1024

MIX_TS = 512
FFN_TS = 512
FFN_COL_CHUNKS = ((0, 768), (768, 1536), (1536, 2304), (2304, 2816))

F32 = jnp.float32
BF16 = jnp.bfloat16
_NT = (((1,), (1,)), ((), ()))
_TN = (((0,), (0,)), ((), ()))


def _rmsnorm(x, g):
    return x * lax.rsqrt(jnp.mean(x * x, axis=-1, keepdims=True) + EPS) * g


def _silu(x):
    return x * jax.nn.sigmoid(x)


def _mixer_kernel(sinks_ref, x_ref, pos_ref, inv_ref, g1_ref, lbl_ref, hg_ref, win_ref, wa_ref, wb_ref,
                  wout_ref, h_ref, st_ref, kbuf, vbuf, qf_s, k_s, lf_s, v_s, oa_s, *, ts):
    t = pl.program_id(1)
    m_s = qf_s
    q_s = lf_s.at[0]
    ob_s = lf_s.at[1]

    @pl.when(t == 0)
    def _():
        st_ref[...] = jnp.zeros_like(st_ref)
        kbuf[0:WINDOW, :] = jnp.zeros((WINDOW, kbuf.shape[1]), BF16)
        vbuf[0:WINDOW, :] = jnp.zeros((WINDOW, vbuf.shape[1]), BF16)

    x = x_ref[0]
    u = _rmsnorm(x, g1_ref[...]).astype(BF16)

    def proj(lo, hi):
        return jnp.dot(u, win_ref[:, lo:hi], preferred_element_type=F32)

    zq = proj(OFF_HQ, OFF_HF)
    qf_s[...] = _silu(zq)
    lbl = lbl_ref[...]
    le = jnp.exp(lbl - jnp.max(lbl, axis=0, keepdims=True))
    lb = le[0:1] / (le[0:1] + le[1:2])
    f = lb + (1.0 - lb) * jax.nn.sigmoid(proj(OFF_HF, OFF_HI))
    k_s[...] = 1.0 - f
    lf = jnp.log(f)
    lf_hi = lf.astype(BF16)
    lf_s[0] = lf_hi
    lf_s[1] = (lf - lf_hi.astype(F32)).astype(BF16)
    v_s[...] = proj(OFF_HI, OFF_HG).astype(BF16)

    row = lax.broadcasted_iota(jnp.int32, (CHUNK, 2 * CHUNK), 0)
    col = lax.broadcasted_iota(jnp.int32, (CHUNK, 2 * CHUNK), 1)
    tri2 = (row >= (col % CHUNK)).astype(BF16)
    causal = (lax.broadcasted_iota(jnp.int32, (CHUNK, CHUNK), 0)
              >= lax.broadcasted_iota(jnp.int32, (CHUNK, CHUNK), 1))
    hg = hg_ref[...]

    def chunk_body(c, carry):
        rows = pl.ds(pl.multiple_of(c * CHUNK, CHUNK), CHUNK)
        l2 = jnp.concatenate([lf_s[0, rows, :], lf_s[1, rows, :]], axis=0)
        b = jnp.dot(tri2, l2, preferred_element_type=F32)
        bref = b[CHUNK // 2:CHUNK // 2 + 1]
        blast = b[CHUNK - 1:CHUNK]
        q_in = qf_s[rows, :] * jnp.exp(b - bref)
        k_in = k_s[rows, :] * jnp.exp(bref - b)
        q_out = (q_in * jnp.exp(bref)).astype(BF16)
        k_st = (k_in * jnp.exp(blast - bref)).astype(BF16)
        q_in = q_in.astype(BF16)
        k_in = k_in.astype(BF16)
        dec = jnp.exp(blast)
        vv = v_s[rows, :]
        for h in range(HGRN_HEADS):
            sl = slice(h * HGRN_DK, (h + 1) * HGRN_DK)
            a = lax.dot_general(q_in[:, sl], k_in[:, sl], _NT, preferred_element_type=F32)
            a = jnp.where(causal, a, 0.0).astype(BF16)
            st = st_ref[h]
            o = (jnp.dot(a, vv[:, sl], preferred_element_type=F32)
                 + lax.dot_general(q_out[:, sl], st.astype(BF16), _NT, preferred_element_type=F32))
            st_ref[h] = st * dec[:, sl] + lax.dot_general(vv[:, sl], k_st[:, sl], _TN,
                                                          preferred_element_type=F32)
            oa_s[rows, sl] = _rmsnorm(o, hg)
        return carry

    lax.fori_loop(0, ts // CHUNK, chunk_body, 0)

    oa = (oa_s[...] * _silu(proj(OFF_HG, OFF_AQ))).astype(BF16)
    ya = jnp.dot(oa, wa_ref[...], preferred_element_type=F32)
    m_s[...] = jax.nn.sigmoid(proj(OFF_GA, OFF_GB)) * ya

    pos = pos_ref[0].astype(F32)
    inv = inv_ref[...]
    ang = jnp.concatenate([inv * pos[:, j * LANES:(j + 1) * LANES] for j in range(ts // LANES)], axis=1)
    c8 = jnp.cos(ang)
    s8 = jnp.sin(ang)
    one_pad = jnp.ones((ATT_HD - ROPE_DIM, ts), F32)
    zero_pad = jnp.zeros((ATT_HD - ROPE_DIM, ts), F32)
    z8 = jnp.zeros((ROPE_HALF, ts), F32)
    reps = LANES // ATT_HD
    cos_t = jnp.concatenate([c8, c8, one_pad] * reps, axis=0).T
    sin_lo_t = jnp.concatenate([z8, s8, zero_pad] * reps, axis=0).T
    sin_hi_t = jnp.concatenate([-s8, z8, zero_pad] * reps, axis=0).T

    def rope(xb):
        return (xb * cos_t + pltpu.roll(xb, ROPE_HALF, 1) * sin_lo_t
                + pltpu.roll(xb, LANES - ROPE_HALF, 1) * sin_hi_t)

    zaq = proj(OFF_AQ, OFF_AK)
    scale = ATT_HD ** -0.5
    for i in range(ATT_HEADS * ATT_HD // LANES):
        sl = slice(i * LANES, (i + 1) * LANES)
        q_s[:, sl] = (rope(zaq[:, sl]) * scale).astype(BF16)

    zkv = proj(OFF_AK, OFF_GA)
    lane = lax.broadcasted_iota(jnp.int32, (ts, LANES), 1)
    lo_half = lane < ATT_HD

    def spread(z):
        zr = pltpu.roll(z, ATT_HD, 1)
        return jnp.concatenate([jnp.where(lo_half, z, 0.0), jnp.where(lo_half, 0.0, zr),
                                jnp.where(lo_half, zr, 0.0), jnp.where(lo_half, 0.0, z)], axis=1).astype(BF16)

    kbuf[WINDOW:WINDOW + ts, :] = spread(rope(zkv[:, 0:LANES]))
    vbuf[WINDOW:WINDOW + ts, :] = spread(zkv[:, LANES:2 * LANES])

    qi = lax.broadcasted_iota(jnp.int32, (WINDOW, 2 * WINDOW), 0)
    mi = lax.broadcasted_iota(jnp.int32, (WINDOW, 2 * WINDOW), 1)
    band = (mi > qi) & (mi <= qi + WINDOW)
    band_first = band & ((mi >= WINDOW) | (t > 0))
    lane_w = lax.broadcasted_iota(jnp.int32, (WINDOW, LANES), 1)
    pairs = ATT_GROUP // 2

    for qb in range(ts // WINDOW):
        qrows = slice(qb * WINDOW, (qb + 1) * WINDOW)
        krows = slice(qb * WINDOW, qb * WINDOW + 2 * WINDOW)
        mask = band_first if qb == 0 else band
        for kv in range(ATT_KV_HEADS):
            c0 = kv * 2 * LANES
            k2 = jnp.concatenate([kbuf[krows, c0:c0 + LANES], kbuf[krows, c0 + LANES:c0 + 2 * LANES]], axis=0)
            v2 = jnp.concatenate([vbuf[krows, c0:c0 + LANES], vbuf[krows, c0 + LANES:c0 + 2 * LANES]], axis=0)
            qs = jnp.concatenate(
                [q_s[qrows, (kv * pairs + i) * LANES:(kv * pairs + i + 1) * LANES] for i in range(pairs)], axis=0)
            s = lax.dot_general(qs, k2, _NT, preferred_element_type=F32)
            p_rows = []
            rden = []
            for i in range(pairs):
                p_cols = []
                for e in range(2):
                    blk = s[i * WINDOW:(i + 1) * WINDOW, e * 2 * WINDOW:(e + 1) * 2 * WINDOW]
                    blk = jnp.where(mask, blk, NEG_INF)
                    sink = sinks_ref[kv * ATT_GROUP + 2 * i + e]
                    m = jnp.maximum(jnp.max(blk, axis=-1, keepdims=True), sink)
                    p = jnp.exp(blk - m)
                    den = jnp.sum(p, axis=-1, keepdims=True) + jnp.exp(sink - m)
                    rden.append(1.0 / den)
                    p_cols.append(p.astype(BF16))
                p_rows.append(jnp.concatenate(p_cols, axis=1))
            p = jnp.concatenate(p_rows, axis=0)
            o = jnp.dot(p, v2, preferred_element_type=F32)
            for i in range(pairs):
                r = jnp.where(lane_w < ATT_HD, rden[2 * i], rden[2 * i + 1])
                ob_s[qrows, (kv * pairs + i) * LANES:(kv * pairs + i + 1) * LANES] = (
                    o[i * WINDOW:(i + 1) * WINDOW] * r).astype(BF16)

    kbuf[0:WINDOW, :] = kbuf[ts:ts + WINDOW, :]
    vbuf[0:WINDOW, :] = vbuf[ts:ts + WINDOW, :]

    yb = jnp.dot(ob_s[...], wb_ref[...], preferred_element_type=F32)
    merged = (m_s[...] + jax.nn.sigmoid(proj(OFF_GB, D_IN)) * yb).astype(BF16)
    h_ref[0] = x_ref[0] + jnp.dot(merged, wout_ref[...], preferred_element_type=F32)


def _ffn_kernel(h_ref, g2_ref, win_ref, cw_ref, cb_ref, wd_ref, fg_ref, o_ref, gbuf, *, ts):
    t = pl.program_id(1)

    @pl.when(t == 0)
    def _():
        gbuf[0:SUBLANES, :] = jnp.zeros((SUBLANES, D_FF), F32)

    h = h_ref[0]
    u = _rmsnorm(h, g2_ref[...]).astype(BF16)
    y = None
    for lo, hi in FFN_COL_CHUNKS:
        g = jnp.dot(u, win_ref[:, lo:hi], preferred_element_type=F32)
        up = jnp.dot(u, win_ref[:, D_FF + lo:D_FF + hi], preferred_element_type=F32)
        gbuf[SUBLANES:SUBLANES + ts, lo:hi] = g
        a = (cw_ref[2:3, lo:hi] * g
             + cw_ref[1:2, lo:hi] * gbuf[SUBLANES - 1:SUBLANES - 1 + ts, lo:hi]
             + cw_ref[0:1, lo:hi] * gbuf[SUBLANES - 2:SUBLANES - 2 + ts, lo:hi]
             + cb_ref[:, lo:hi])
        act = (_silu(a) * up).astype(BF16)
        part = jnp.dot(act, wd_ref[lo:hi, :], preferred_element_type=F32)
        y = part if y is None else y + part
    gbuf[0:SUBLANES, :] = gbuf[ts:ts + SUBLANES, :]
    o_ref[0] = _rmsnorm(h + y, fg_ref[...])


def _const_spec(shape):
    return pl.BlockSpec(shape, lambda b, t: (0,) * len(shape), pipeline_mode=pl.Buffered(1))


def _mixer_call(x, pos3, inv_tab, g1, lbl, hg, w_in, w_a, w_b, w_out, sinks):
    B, S, D = x.shape
    ts = MIX_TS
    assert S % ts == 0 and ts % WINDOW == 0 and ts % CHUNK == 0 and ts % LANES == 0
    tile = pl.BlockSpec((1, ts, D), lambda b, t: (b, t, 0))
    return pl.pallas_call(
        functools.partial(_mixer_kernel, ts=ts),
        grid=(B, S // ts),
        in_specs=[
            pl.BlockSpec(memory_space=pltpu.SMEM),
            tile,
            pl.BlockSpec((1, 1, ts), lambda b, t: (b, 0, t)),
            _const_spec(inv_tab.shape),
            _const_spec(g1.shape),
            _const_spec(lbl.shape),
            _const_spec(hg.shape),
            _const_spec(w_in.shape),
            _const_spec(w_a.shape),
            _const_spec(w_b.shape),
            _const_spec(w_out.shape),
        ],
        out_specs=tile,
        out_shape=jax.ShapeDtypeStruct((B, S, D), F32),
        scratch_shapes=[
            pltpu.VMEM((HGRN_HEADS, HGRN_DV, HGRN_DK), F32),
            pltpu.VMEM((WINDOW + ts, 4 * LANES), BF16),
            pltpu.VMEM((WINDOW + ts, 4 * LANES), BF16),
            pltpu.VMEM((ts, HGRN_F), F32),
            pltpu.VMEM((ts, HGRN_F), F32),
            pltpu.VMEM((2, ts, HGRN_F), BF16),
            pltpu.VMEM((ts, HGRN_V), BF16),
            pltpu.VMEM((ts, HGRN_V), F32),
        ],
        compiler_params=pltpu.CompilerParams(
            dimension_semantics=("parallel", "arbitrary"), vmem_limit_bytes=VMEM_LIMIT_BYTES),
        name="token_mixer",
    )(sinks, x, pos3, inv_tab, g1, lbl, hg, w_in, w_a, w_b, w_out)


def _ffn_call(h, g2, w_ffn_in, conv_w, conv_b, w_down, fg):
    B, S, D = h.shape
    ts = FFN_TS
    assert S % ts == 0 and ts % SUBLANES == 0
    tile = pl.BlockSpec((1, ts, D), lambda b, t: (b, t, 0))
    return pl.pallas_call(
        functools.partial(_ffn_kernel, ts=ts),
        grid=(B, S // ts),
        in_specs=[
            tile,
            _const_spec(g2.shape),
            _const_spec(w_ffn_in.shape),
            _const_spec(conv_w.shape),
            _const_spec(conv_b.shape),
            _const_spec(w_down.shape),
            _const_spec(fg.shape),
        ],
        out_specs=tile,
        out_shape=jax.ShapeDtypeStruct((B, S, D), F32),
        scratch_shapes=[pltpu.VMEM((SUBLANES + ts, D_FF), F32)],
        compiler_params=pltpu.CompilerParams(
            dimension_semantics=("parallel", "arbitrary"), vmem_limit_bytes=VMEM_LIMIT_BYTES),
        name="channel_mixer",
    )(h, g2, w_ffn_in, conv_w, conv_b, w_down, fg)


def kernel(x, positions, norm1_g, w_in, lb_logits, hgrn_norm_g, w_a, attn_sinks, w_b, w_out, norm2_g, w_ffn_in,
           conv_w, conv_b, w_down, final_g):
    B, S, D = x.shape
    assert norm1_g.shape[0] == 1, "single-layer stack"
    pos3 = positions.reshape(B, 1, S)
    inv = ROPE_THETA ** (-2.0 * jnp.arange(ROPE_HALF, dtype=F32) / ROPE_DIM)
    inv_tab = jnp.broadcast_to(inv[:, None], (ROPE_HALF, LANES))
    h = _mixer_call(
        x, pos3, inv_tab, norm1_g[0][None, :], lb_logits.astype(F32), hgrn_norm_g[0][None, :],
        w_in[0].astype(BF16), w_a[0].astype(BF16), w_b[0].astype(BF16), w_out[0].astype(BF16), attn_sinks[0])
    return _ffn_call(
        h, norm2_g[0][None, :], w_ffn_in[0].astype(BF16), conv_w[0], conv_b[0][None, :],
        w_down[0].astype(BF16), final_g[None, :])
```

```python
import functools

import jax
import jax.numpy as jnp
from jax import lax
from jax.experimental import pallas as pl
from jax.experimental.pallas import tpu as pltpu

D_MODEL = 1024
HGRN_HEADS = 8
HGRN_DK = 128
HGRN_DV = 128
HGRN_F = HGRN_HEADS * HGRN_DK
HGRN_V = HGRN_HEADS * HGRN_DV
CHUNK = 64
ATT_HEADS = 16
ATT_KV_HEADS = 2
ATT_HD = 64
ATT_GROUP = ATT_HEADS // ATT_KV_HEADS
WINDOW = 128
ROPE_DIM = ATT_HD // 4
ROPE_HALF = ROPE_DIM // 2
ROPE_THETA = 500000.0
D_FF = 2816
CONV_W = 3
EPS = 1e-6
NEG_INF = -1e30
LOG2_E = 1.4426950408889634

OFF_HQ = 0
OFF_HF = OFF_HQ + HGRN_F
OFF_HI = OFF_HF + HGRN_F
OFF_HG = OFF_HI + HGRN_V
OFF_AQ = OFF_HG + HGRN_V
OFF_AK = OFF_AQ + ATT_HEADS * ATT_HD
OFF_AV = OFF_AK + ATT_KV_HEADS * ATT_HD
OFF_GA = OFF_AV + ATT_KV_HEADS * ATT_HD
OFF_GB = OFF_GA + D_MODEL
D_IN = OFF_GB + D_MODEL

LANES = 128
SUBLANES = 8
VMEM_LIMIT_BYTES = 58 * 1024 * 1024

MIX_TS = 256
FFN_TS = 512
FFN_COL_CHUNKS = ((0, 768), (768, 1536), (1536, 2304), (2304, 2816))

F32 = jnp.float32
BF16 = jnp.bfloat16
_NT = (((1,), (1,)), ((), ()))
_TN = (((0,), (0,)), ((), ()))


def _rmsnorm(x, g):
    return x * lax.rsqrt(jnp.mean(x * x, axis=-1, keepdims=True) + EPS) * g


def _sigmoid(x):
    return 0.5 * jnp.tanh(0.5 * x) + 0.5


def _silu(x):
    return x * _sigmoid(x)


def _mixer_kernel(sinks_ref, x_ref, pos_ref, inv_ref, g1_ref, lbl_ref, hg_ref, win_ref, wa_ref, wb_ref,
                  wout_ref, h_ref, st_ref, kbuf, vbuf, qf_s, k_s, lf_s, v_s, b_s, qi_s, ki_s, ks_s, qo_s,
                  u_s, oa_s, gate_s, ga_s, gb_s, q_s, p_s, ob_s, *, ts):
    t = pl.program_id(1)
    n_chunks = ts // CHUNK
    n_qb = ts // WINDOW
    heads = [slice(h * HGRN_DK, (h + 1) * HGRN_DK) for h in range(HGRN_HEADS)]
    pairs = ATT_GROUP // 2

    @pl.when(t == 0)
    def _():
        st_ref[...] = jnp.zeros_like(st_ref)
        kbuf[0:WINDOW, :] = jnp.zeros((WINDOW, kbuf.shape[1]), BF16)
        vbuf[0:WINDOW, :] = jnp.zeros((WINDOW, vbuf.shape[1]), BF16)

    u = _rmsnorm(x_ref[0], g1_ref[...]).astype(BF16)

    def proj(lo, hi):
        return jnp.dot(u, win_ref[:, lo:hi], preferred_element_type=F32)

    def hgrn_q():
        qf_s[...] = _silu(proj(OFF_HQ, OFF_HF))

    def hgrn_f():
        lbl = lbl_ref[...]
        le = jnp.exp(lbl - jnp.max(lbl, axis=0, keepdims=True))
        lb = le[0:1] / (le[0:1] + le[1:2])
        f = lb + (1.0 - lb) * _sigmoid(proj(OFF_HF, OFF_HI))
        k_s[...] = 1.0 - f
        lf = jnp.log(f)
        lf_hi = lf.astype(BF16)
        lf_s[0] = lf_hi
        lf_s[1] = (lf - lf_hi.astype(F32)).astype(BF16)

    def hgrn_v():
        v_s[...] = proj(OFF_HI, OFF_HG).astype(BF16)

    def hgrn_gate():
        gate_s[...] = _silu(proj(OFF_HG, OFF_AQ))

    row = lax.broadcasted_iota(jnp.int32, (CHUNK, 2 * CHUNK), 0)
    col = lax.broadcasted_iota(jnp.int32, (CHUNK, 2 * CHUNK), 1)
    tri2 = (row >= (col % CHUNK)).astype(BF16)
    causal = (lax.broadcasted_iota(jnp.int32, (CHUNK, CHUNK), 0)
              >= lax.broadcasted_iota(jnp.int32, (CHUNK, CHUNK), 1))
    hg = hg_ref[...]

    def crows(c):
        return slice(c * CHUNK, (c + 1) * CHUNK)

    def hgrn_cumsum(c):
        rows = crows(c)
        l2 = jnp.concatenate([lf_s[0, rows, :], lf_s[1, rows, :]], axis=0)
        b_s[rows, :] = jnp.dot(tri2, l2, preferred_element_type=F32)

    dec = [None] * n_chunks

    def hgrn_prep(c):
        rows = crows(c)
        b = b_s[rows, :]
        bref = b[CHUNK // 2:CHUNK // 2 + 1]
        blast = b[CHUNK - 1:CHUNK]
        q_in = qf_s[rows, :] * jnp.exp(b - bref)
        k_in = k_s[rows, :] * jnp.exp(bref - b)
        qo_s[rows, :] = (q_in * jnp.exp(bref)).astype(BF16)
        ks_s[rows, :] = (k_in * jnp.exp(blast - bref)).astype(BF16)
        qi_s[rows, :] = q_in.astype(BF16)
        ki_s[rows, :] = k_in.astype(BF16)
        dec[c] = jnp.exp(blast)

    scores = [None] * n_chunks

    def hgrn_scores(c):
        rows = crows(c)
        scores[c] = [lax.dot_general(qi_s[rows, sl], ki_s[rows, sl], _NT, preferred_element_type=F32)
                     for sl in heads]

    def hgrn_increment(c):
        rows = crows(c)
        for h, sl in enumerate(heads):
            u_s[c, h] = lax.dot_general(v_s[rows, sl], ks_s[rows, sl], _TN, preferred_element_type=F32)

    def hgrn_intra(c):
        rows = crows(c)
        for h, sl in enumerate(heads):
            a = jnp.where(causal, scores[c][h], 0.0).astype(BF16)
            oa_s[rows, sl] = jnp.dot(a, v_s[rows, sl], preferred_element_type=F32)

    state = [None] * HGRN_HEADS

    def hgrn_recur(c):
        rows = crows(c)
        for h, sl in enumerate(heads):
            if c == 0:
                state[h] = st_ref[h]
            o = oa_s[rows, sl] + lax.dot_general(qo_s[rows, sl], state[h].astype(BF16), _NT,
                                                 preferred_element_type=F32)
            state[h] = state[h] * dec[c][:, sl] + u_s[c, h]
            oa_s[rows, sl] = _rmsnorm(o, hg)

    pos = pos_ref[0].astype(F32)
    inv = inv_ref[...]
    ang = jnp.concatenate([inv * pos[:, j * LANES:(j + 1) * LANES] for j in range(ts // LANES)], axis=1)
    c8 = jnp.cos(ang)
    s8 = jnp.sin(ang)
    one_pad = jnp.ones((ATT_HD - ROPE_DIM, ts), F32)
    zero_pad = jnp.zeros((ATT_HD - ROPE_DIM, ts), F32)
    z8 = jnp.zeros((ROPE_HALF, ts), F32)
    reps = LANES // ATT_HD
    cos_t = jnp.concatenate([c8, c8, one_pad] * reps, axis=0).T
    sin_lo_t = jnp.concatenate([z8, s8, zero_pad] * reps, axis=0).T
    sin_hi_t = jnp.concatenate([-s8, z8, zero_pad] * reps, axis=0).T

    def rope(xb):
        return (xb * cos_t + pltpu.roll(xb, ROPE_HALF, 1) * sin_lo_t
                + pltpu.roll(xb, LANES - ROPE_HALF, 1) * sin_hi_t)

    def att_q():
        zaq = proj(OFF_AQ, OFF_AK)
        scale = ATT_HD ** -0.5 * LOG2_E
        for i in range(ATT_HEADS * ATT_HD // LANES):
            sl = slice(i * LANES, (i + 1) * LANES)
            q_s[:, sl] = (rope(zaq[:, sl]) * scale).astype(BF16)

    def att_kv():
        zkv = proj(OFF_AK, OFF_GA)
        lo_half = lax.broadcasted_iota(jnp.int32, (ts, LANES), 1) < ATT_HD

        def spread(z):
            zr = pltpu.roll(z, ATT_HD, 1)
            return jnp.concatenate([jnp.where(lo_half, z, 0.0), jnp.where(lo_half, 0.0, zr),
                                    jnp.where(lo_half, zr, 0.0), jnp.where(lo_half, 0.0, z)],
                                   axis=1).astype(BF16)

        kbuf[WINDOW:WINDOW + ts, :] = spread(rope(zkv[:, 0:LANES]))
        vbuf[WINDOW:WINDOW + ts, :] = spread(zkv[:, LANES:2 * LANES])

    qi = lax.broadcasted_iota(jnp.int32, (WINDOW, 2 * WINDOW), 0)
    mi = lax.broadcasted_iota(jnp.int32, (WINDOW, 2 * WINDOW), 1)
    band = (mi > qi) & (mi <= qi + WINDOW)
    band_first = band & ((mi >= WINDOW) | (t > 0))

    def band_rows(buf, qb, kv):
        krows = slice(qb * WINDOW, qb * WINDOW + 2 * WINDOW)
        c0 = kv * 2 * LANES
        return jnp.concatenate([buf[krows, c0:c0 + LANES], buf[krows, c0 + LANES:c0 + 2 * LANES]], axis=0)

    def att_softmax(qb, kv):
        qrows = slice(qb * WINDOW, (qb + 1) * WINDOW)
        mask = band_first if qb == 0 else band
        qs = jnp.concatenate(
            [q_s[qrows, (kv * pairs + i) * LANES:(kv * pairs + i + 1) * LANES] for i in range(pairs)], axis=0)
        s = lax.dot_general(qs, band_rows(kbuf, qb, kv), _NT, preferred_element_type=F32)
        for i in range(pairs):
            for e in range(2):
                blk = s[i * WINDOW:(i + 1) * WINDOW, e * 2 * WINDOW:(e + 1) * 2 * WINDOW]
                blk = jnp.where(mask, blk, NEG_INF)
                sink = sinks_ref[kv * ATT_GROUP + 2 * i + e] * LOG2_E
                m = jnp.maximum(jnp.max(blk, axis=-1, keepdims=True), sink)
                p = jnp.exp2(blk - m)
                den = jnp.sum(p, axis=-1, keepdims=True) + jnp.exp2(sink - m)
                p_s[qb * ATT_KV_HEADS + kv, i * WINDOW:(i + 1) * WINDOW,
                    e * 2 * WINDOW:(e + 1) * 2 * WINDOW] = (p * (1.0 / den)).astype(BF16)

    def att_values(qb, kv):
        qrows = slice(qb * WINDOW, (qb + 1) * WINDOW)
        o = jnp.dot(p_s[qb * ATT_KV_HEADS + kv], band_rows(vbuf, qb, kv), preferred_element_type=F32)
        for i in range(pairs):
            ob_s[qrows, (kv * pairs + i) * LANES:(kv * pairs + i + 1) * LANES] = (
                o[i * WINDOW:(i + 1) * WINDOW].astype(BF16))

    def gate_a():
        ga_s[...] = _sigmoid(proj(OFF_GA, OFF_GB))

    def gate_b():
        gb_s[...] = _sigmoid(proj(OFF_GB, D_IN))

    hgrn_q()
    hgrn_f()
    hgrn_v()
    for c in range(n_chunks):
        hgrn_cumsum(c)
    att_q()
    att_kv()
    hgrn_prep(0)
    hgrn_gate()
    hgrn_prep(1)
    fillers = [gate_a, gate_b]
    att_jobs = [(qb, kv) for qb in range(n_qb) for kv in range(ATT_KV_HEADS)]
    soft_done = 0
    for c in range(n_chunks):
        hgrn_scores(c)
        hgrn_increment(c)
        if c >= 1:
            hgrn_intra(c - 1)
        if c + 2 < n_chunks:
            hgrn_prep(c + 2)
        if fillers:
            fillers.pop(0)()
        elif soft_done < len(att_jobs):
            att_softmax(*att_jobs[soft_done])
            soft_done += 1
    while fillers:
        fillers.pop(0)()
    while soft_done < min(2, len(att_jobs)):
        att_softmax(*att_jobs[soft_done])
        soft_done += 1
    hgrn_intra(n_chunks - 1)
    vals_done = 0
    for c in range(n_chunks):
        hgrn_recur(c)
        if soft_done < len(att_jobs):
            att_softmax(*att_jobs[soft_done])
            soft_done += 1
        if vals_done < soft_done - 1:
            att_values(*att_jobs[vals_done])
            vals_done += 1
    while soft_done < len(att_jobs):
        att_softmax(*att_jobs[soft_done])
        soft_done += 1
    for h in range(HGRN_HEADS):
        st_ref[h] = state[h]
    oa = (oa_s[...] * gate_s[...]).astype(BF16)
    ya = jnp.dot(oa, wa_ref[...], preferred_element_type=F32)
    while vals_done < len(att_jobs):
        att_values(*att_jobs[vals_done])
        vals_done += 1
    kbuf[0:WINDOW, :] = kbuf[ts:ts + WINDOW, :]
    vbuf[0:WINDOW, :] = vbuf[ts:ts + WINDOW, :]
    yb = jnp.dot(ob_s[...], wb_ref[...], preferred_element_type=F32)
    merged = (ga_s[...] * ya + gb_s[...] * yb).astype(BF16)
    h_ref[0] = x_ref[0] + jnp.dot(merged, wout_ref[...], preferred_element_type=F32)


def _ffn_kernel(h_ref, g2_ref, win_ref, cw_ref, cb_ref, wd_ref, fg_ref, o_ref, gbuf, *, ts):
    t = pl.program_id(1)

    @pl.when(t == 0)
    def _():
        gbuf[0:SUBLANES, :] = jnp.zeros((SUBLANES, D_FF), F32)

    h = h_ref[0]
    u = _rmsnorm(h, g2_ref[...]).astype(BF16)
    y = None
    for lo, hi in FFN_COL_CHUNKS:
        g = jnp.dot(u, win_ref[:, lo:hi], preferred_element_type=F32)
        up = jnp.dot(u, win_ref[:, D_FF + lo:D_FF + hi], preferred_element_type=F32)
        gbuf[SUBLANES:SUBLANES + ts, lo:hi] = g
        a = (cw_ref[2:3, lo:hi] * g
             + cw_ref[1:2, lo:hi] * gbuf[SUBLANES - 1:SUBLANES - 1 + ts, lo:hi]
             + cw_ref[0:1, lo:hi] * gbuf[SUBLANES - 2:SUBLANES - 2 + ts, lo:hi]
             + cb_ref[:, lo:hi])
        act = (_silu(a) * up).astype(BF16)
        part = jnp.dot(act, wd_ref[lo:hi, :], preferred_element_type=F32)
        y = part if y is None else y + part
    gbuf[0:SUBLANES, :] = gbuf[ts:ts + SUBLANES, :]
    o_ref[0] = _rmsnorm(h + y, fg_ref[...])


def _const_spec(shape):
    return pl.BlockSpec(shape, lambda b, t: (0,) * len(shape), pipeline_mode=pl.Buffered(1))


def _mixer_call(x, pos3, inv_tab, g1, lbl, hg, w_in, w_a, w_b, w_out, sinks):
    B, S, D = x.shape
    ts = MIX_TS
    assert S % ts == 0 and ts % WINDOW == 0 and ts % CHUNK == 0 and ts % LANES == 0
    tile = pl.BlockSpec((1, ts, D), lambda b, t: (b, t, 0))
    wide_f32 = pltpu.VMEM((ts, D_MODEL), F32)
    wide_bf16 = pltpu.VMEM((ts, D_MODEL), BF16)
    return pl.pallas_call(
        functools.partial(_mixer_kernel, ts=ts),
        grid=(B, S // ts),
        in_specs=[
            pl.BlockSpec(memory_space=pltpu.SMEM),
            tile,
            pl.BlockSpec((1, 1, ts), lambda b, t: (b, 0, t)),
            _const_spec(inv_tab.shape),
            _const_spec(g1.shape),
            _const_spec(lbl.shape),
            _const_spec(hg.shape),
            _const_spec(w_in.shape),
            _const_spec(w_a.shape),
            _const_spec(w_b.shape),
            _const_spec(w_out.shape),
        ],
        out_specs=tile,
        out_shape=jax.ShapeDtypeStruct((B, S, D), F32),
        scratch_shapes=[
            pltpu.VMEM((HGRN_HEADS, HGRN_DV, HGRN_DK), F32),
            pltpu.VMEM((WINDOW + ts, 4 * LANES), BF16),
            pltpu.VMEM((WINDOW + ts, 4 * LANES), BF16),
            wide_f32,
            wide_f32,
            pltpu.VMEM((2, ts, HGRN_F), BF16),
            wide_bf16,
            wide_f32,
            wide_bf16,
            wide_bf16,
            wide_bf16,
            wide_bf16,
            pltpu.VMEM((ts // CHUNK, HGRN_HEADS, HGRN_DV, HGRN_DK), F32),
            wide_f32,
            wide_f32,
            wide_f32,
            wide_f32,
            wide_bf16,
            pltpu.VMEM((ts // WINDOW * ATT_KV_HEADS, ATT_GROUP // 2 * WINDOW, 4 * WINDOW), BF16),
            wide_bf16,
        ],
        compiler_params=pltpu.CompilerParams(
            dimension_semantics=("parallel", "arbitrary"), vmem_limit_bytes=VMEM_LIMIT_BYTES),
        name="token_mixer",
    )(sinks, x, pos3, inv_tab, g1, lbl, hg, w_in, w_a, w_b, w_out)


def _ffn_call(h, g2, w_ffn_in, conv_w, conv_b, w_down, fg):
    B, S, D = h.shape
    ts = FFN_TS
    assert S % ts == 0 and ts % SUBLANES == 0
    tile = pl.BlockSpec((1, ts, D), lambda b, t: (b, t, 0))
    return pl.pallas_call(
        functools.partial(_ffn_kernel, ts=ts),
        grid=(B, S // ts),
        in_specs=[
            tile,
            _const_spec(g2.shape),
            _const_spec(w_ffn_in.shape),
            _const_spec(conv_w.shape),
            _const_spec(conv_b.shape),
            _const_spec(w_down.shape),
            _const_spec(fg.shape),
        ],
        out_specs=tile,
        out_shape=jax.ShapeDtypeStruct((B, S, D), F32),
        scratch_shapes=[pltpu.VMEM((SUBLANES + ts, D_FF), F32)],
        compiler_params=pltpu.CompilerParams(
            dimension_semantics=("parallel", "arbitrary"), vmem_limit_bytes=VMEM_LIMIT_BYTES),
        name="channel_mixer",
    )(h, g2, w_ffn_in, conv_w, conv_b, w_down, fg)


def kernel(x, positions, norm1_g, w_in, lb_logits, hgrn_norm_g, w_a, attn_sinks, w_b, w_out, norm2_g, w_ffn_in,
           conv_w, conv_b, w_down, final_g):
    B, S, D = x.shape
    assert norm1_g.shape[0] == 1, "single-layer stack"
    pos3 = positions.reshape(B, 1, S)
    inv = ROPE_THETA ** (-2.0 * jnp.arange(ROPE_HALF, dtype=F32) / ROPE_DIM)
    inv_tab = jnp.broadcast_to(inv[:, None], (ROPE_HALF, LANES))
    h = _mixer_call(
        x, pos3, inv_tab, norm1_g[0][None, :], lb_logits.astype(F32), hgrn_norm_g[0][None, :],
        w_in[0].astype(BF16), w_a[0].astype(BF16), w_b[0].astype(BF16), w_out[0].astype(BF16), attn_sinks[0])
    return _ffn_call(
        h, norm2_g[0][None, :], w_ffn_in[0].astype(BF16), conv_w[0], conv_b[0][None, :],
        w_down[0].astype(BF16), final_g[None, :])
```

```python
import functools

import jax
import jax.numpy as jnp
from jax import lax
from jax.experimental import pallas as pl
from jax.experimental.pallas import tpu as pltpu

D_MODEL = 1024
HGRN_HEADS = 8
HGRN_DK = 128
HGRN_DV = 128
HGRN_F = HGRN_HEADS * HGRN_DK
HGRN_V = HGRN_HEADS * HGRN_DV
CHUNK = 64
ATT_HEADS = 16
ATT_KV_HEADS = 2
ATT_HD = 64
ATT_GROUP = ATT_HEADS // ATT_KV_HEADS
WINDOW = 128
ROPE_DIM = ATT_HD // 4
ROPE_HALF = ROPE_DIM // 2
ROPE_THETA = 500000.0
D_FF = 2816
CONV_W = 3
EPS = 1e-6
NEG_INF = -1e30
LOG2_E = 1.4426950408889634

OFF_HQ = 0
OFF_HF = OFF_HQ + HGRN_F
OFF_HI = OFF_HF + HGRN_F
OFF_HG = OFF_HI + HGRN_V
OFF_AQ = OFF_HG + HGRN_V
OFF_AK = OFF_AQ + ATT_HEADS * ATT_HD
OFF_AV = OFF_AK + ATT_KV_HEADS * ATT_HD
OFF_GA = OFF_AV + ATT_KV_HEADS * ATT_HD
OFF_GB = OFF_GA + D_MODEL
D_IN = OFF_GB + D_MODEL

LANES = 128
SUBLANES = 8
VMEM_LIMIT_BYTES = 58 * 1024 * 1024

MIX_TS = 256
FFN_TS = 512
FFN_COL_CHUNKS = ((0, 768), (768, 1536), (1536, 2304), (2304, 2816))

F32 = jnp.float32
BF16 = jnp.bfloat16
_NT = (((1,), (1,)), ((), ()))
_TN = (((0,), (0,)), ((), ()))


def _rmsnorm(x, g):
    return x * lax.rsqrt(jnp.mean(x * x, axis=-1, keepdims=True) + EPS) * g


def _sigmoid(x):
    return 0.5 * jnp.tanh(0.5 * x) + 0.5


def _silu(x):
    return x * _sigmoid(x)


def _mixer_kernel(sinks_ref, x_ref, xprev_ref, pos_ref, inv_ref, g1_ref, lbl_ref, hg_ref, win_ref, wa_ref,
                  wb_ref, wout_ref, h_ref, st_ref, kbuf, vbuf, qf_s, k_s, lf_s, v_s, b_s, qi_s, ki_s, ks_s, qo_s,
                  u_s, oa_s, gate_s, ga_s, gb_s, q_s, p_s, ob_s, oa_bf, mg_s, *, ts, n_tiles, tiles_per_seq):
    i = pl.program_id(0)
    t = jnp.minimum(i, n_tiles - 1) % tiles_per_seq
    n_chunks = ts // CHUNK
    n_qb = ts // WINDOW
    heads = [slice(h * HGRN_DK, (h + 1) * HGRN_DK) for h in range(HGRN_HEADS)]
    pairs = ATT_GROUP // 2

    @pl.when(i == 0)
    def _():
        oa_bf[...] = jnp.zeros_like(oa_bf)
        ob_s[...] = jnp.zeros_like(ob_s)
        ga_s[...] = jnp.zeros_like(ga_s)
        gb_s[...] = jnp.zeros_like(gb_s)

    @pl.when(t == 0)
    def _():
        st_ref[...] = jnp.zeros_like(st_ref)
        kbuf[0:WINDOW, :] = jnp.zeros((WINDOW, kbuf.shape[1]), BF16)
        vbuf[0:WINDOW, :] = jnp.zeros((WINDOW, vbuf.shape[1]), BF16)

    branch_out = {}

    def tail_ya():
        branch_out["a"] = jnp.dot(oa_bf[...], wa_ref[...], preferred_element_type=F32)

    def tail_yb():
        branch_out["b"] = jnp.dot(ob_s[...], wb_ref[...], preferred_element_type=F32)

    def tail_merge():
        mg_s[...] = (ga_s[...] * branch_out["a"] + gb_s[...] * branch_out["b"]).astype(BF16)

    def tail_out():
        h_ref[0] = xprev_ref[0] + jnp.dot(mg_s[...], wout_ref[...], preferred_element_type=F32)

    tail_ya()
    tail_yb()
    u = _rmsnorm(x_ref[0], g1_ref[...]).astype(BF16)

    def proj(lo, hi):
        return jnp.dot(u, win_ref[:, lo:hi], preferred_element_type=F32)

    def hgrn_q():
        qf_s[...] = _silu(proj(OFF_HQ, OFF_HF))

    def hgrn_f():
        lbl = lbl_ref[...]
        le = jnp.exp(lbl - jnp.max(lbl, axis=0, keepdims=True))
        lb = le[0:1] / (le[0:1] + le[1:2])
        f = lb + (1.0 - lb) * _sigmoid(proj(OFF_HF, OFF_HI))
        k_s[...] = 1.0 - f
        lf = jnp.log(f)
        lf_hi = lf.astype(BF16)
        lf_s[0] = lf_hi
        lf_s[1] = (lf - lf_hi.astype(F32)).astype(BF16)

    def hgrn_v():
        v_s[...] = proj(OFF_HI, OFF_HG).astype(BF16)

    def hgrn_gate():
        gate_s[...] = _silu(proj(OFF_HG, OFF_AQ))

    row = lax.broadcasted_iota(jnp.int32, (CHUNK, 2 * CHUNK), 0)
    col = lax.broadcasted_iota(jnp.int32, (CHUNK, 2 * CHUNK), 1)
    tri2 = (row >= (col % CHUNK)).astype(BF16)
    causal = (lax.broadcasted_iota(jnp.int32, (CHUNK, CHUNK), 0)
              >= lax.broadcasted_iota(jnp.int32, (CHUNK, CHUNK), 1))
    hg = hg_ref[...]

    def crows(c):
        return slice(c * CHUNK, (c + 1) * CHUNK)

    def hgrn_cumsum(c):
        rows = crows(c)
        l2 = jnp.concatenate([lf_s[0, rows, :], lf_s[1, rows, :]], axis=0)
        b_s[rows, :] = jnp.dot(tri2, l2, preferred_element_type=F32)

    dec = [None] * n_chunks

    def hgrn_prep(c):
        rows = crows(c)
        b = b_s[rows, :]
        bref = b[CHUNK // 2:CHUNK // 2 + 1]
        blast = b[CHUNK - 1:CHUNK]
        q_in = qf_s[rows, :] * jnp.exp(b - bref)
        k_in = k_s[rows, :] * jnp.exp(bref - b)
        qo_s[rows, :] = (q_in * jnp.exp(bref)).astype(BF16)
        ks_s[rows, :] = (k_in * jnp.exp(blast - bref)).astype(BF16)
        qi_s[rows, :] = q_in.astype(BF16)
        ki_s[rows, :] = k_in.astype(BF16)
        dec[c] = jnp.exp(blast)

    scores = [None] * n_chunks

    def hgrn_scores(c):
        rows = crows(c)
        scores[c] = [lax.dot_general(qi_s[rows, sl], ki_s[rows, sl], _NT, preferred_element_type=F32)
                     for sl in heads]

    def hgrn_increment(c):
        rows = crows(c)
        for h, sl in enumerate(heads):
            u_s[c, h] = lax.dot_general(v_s[rows, sl], ks_s[rows, sl], _TN, preferred_element_type=F32)

    def hgrn_intra(c):
        rows = crows(c)
        for h, sl in enumerate(heads):
            a = jnp.where(causal, scores[c][h], 0.0).astype(BF16)
            oa_s[rows, sl] = jnp.dot(a, v_s[rows, sl], preferred_element_type=F32)

    state = [None] * HGRN_HEADS

    def hgrn_recur(c):
        rows = crows(c)
        for h, sl in enumerate(heads):
            if c == 0:
                state[h] = st_ref[h]
            o = oa_s[rows, sl] + lax.dot_general(qo_s[rows, sl], state[h].astype(BF16), _NT,
                                                 preferred_element_type=F32)
            state[h] = state[h] * dec[c][:, sl] + u_s[c, h]
            oa_s[rows, sl] = _rmsnorm(o, hg)

    pos = pos_ref[0].astype(F32)
    inv = inv_ref[...]
    ang = jnp.concatenate([inv * pos[:, j * LANES:(j + 1) * LANES] for j in range(ts // LANES)], axis=1)
    c8 = jnp.cos(ang)
    s8 = jnp.sin(ang)
    one_pad = jnp.ones((ATT_HD - ROPE_DIM, ts), F32)
    zero_pad = jnp.zeros((ATT_HD - ROPE_DIM, ts), F32)
    z8 = jnp.zeros((ROPE_HALF, ts), F32)
    reps = LANES // ATT_HD
    cos_t = jnp.concatenate([c8, c8, one_pad] * reps, axis=0).T
    sin_lo_t = jnp.concatenate([z8, s8, zero_pad] * reps, axis=0).T
    sin_hi_t = jnp.concatenate([-s8, z8, zero_pad] * reps, axis=0).T

    def rope(xb):
        return (xb * cos_t + pltpu.roll(xb, ROPE_HALF, 1) * sin_lo_t
                + pltpu.roll(xb, LANES - ROPE_HALF, 1) * sin_hi_t)

    def att_q():
        zaq = proj(OFF_AQ, OFF_AK)
        scale = ATT_HD ** -0.5 * LOG2_E
        for i in range(ATT_HEADS * ATT_HD // LANES):
            sl = slice(i * LANES, (i + 1) * LANES)
            q_s[:, sl] = (rope(zaq[:, sl]) * scale).astype(BF16)

    def att_kv():
        zkv = proj(OFF_AK, OFF_GA)
        lo_half = lax.broadcasted_iota(jnp.int32, (ts, LANES), 1) < ATT_HD

        def spread(z):
            zr = pltpu.roll(z, ATT_HD, 1)
            return jnp.concatenate([jnp.where(lo_half, z, 0.0), jnp.where(lo_half, 0.0, zr),
                                    jnp.where(lo_half, zr, 0.0), jnp.where(lo_half, 0.0, z)],
                                   axis=1).astype(BF16)

        kbuf[WINDOW:WINDOW + ts, :] = spread(rope(zkv[:, 0:LANES]))
        vbuf[WINDOW:WINDOW + ts, :] = spread(zkv[:, LANES:2 * LANES])

    qi = lax.broadcasted_iota(jnp.int32, (WINDOW, 2 * WINDOW), 0)
    mi = lax.broadcasted_iota(jnp.int32, (WINDOW, 2 * WINDOW), 1)
    band = (mi > qi) & (mi <= qi + WINDOW)
    band_first = band & ((mi >= WINDOW) | (t > 0))

    def band_rows(buf, qb, kv):
        krows = slice(qb * WINDOW, qb * WINDOW + 2 * WINDOW)
        c0 = kv * 2 * LANES
        return jnp.concatenate([buf[krows, c0:c0 + LANES], buf[krows, c0 + LANES:c0 + 2 * LANES]], axis=0)

    def att_softmax(qb, kv):
        qrows = slice(qb * WINDOW, (qb + 1) * WINDOW)
        mask = band_first if qb == 0 else band
        qs = jnp.concatenate(
            [q_s[qrows, (kv * pairs + i) * LANES:(kv * pairs + i + 1) * LANES] for i in range(pairs)], axis=0)
        s = lax.dot_general(qs, band_rows(kbuf, qb, kv), _NT, preferred_element_type=F32)
        for i in range(pairs):
            for e in range(2):
                blk = s[i * WINDOW:(i + 1) * WINDOW, e * 2 * WINDOW:(e + 1) * 2 * WINDOW]
                blk = jnp.where(mask, blk, NEG_INF)
                sink = sinks_ref[kv * ATT_GROUP + 2 * i + e] * LOG2_E
                m = jnp.maximum(jnp.max(blk, axis=-1, keepdims=True), sink)
                p = jnp.exp2(blk - m)
                den = jnp.sum(p, axis=-1, keepdims=True) + jnp.exp2(sink - m)
                p_s[qb * ATT_KV_HEADS + kv, i * WINDOW:(i + 1) * WINDOW,
                    e * 2 * WINDOW:(e + 1) * 2 * WINDOW] = (p * (1.0 / den)).astype(BF16)

    def att_values(qb, kv):
        qrows = slice(qb * WINDOW, (qb + 1) * WINDOW)
        o = jnp.dot(p_s[qb * ATT_KV_HEADS + kv], band_rows(vbuf, qb, kv), preferred_element_type=F32)
        for i in range(pairs):
            ob_s[qrows, (kv * pairs + i) * LANES:(kv * pairs + i + 1) * LANES] = (
                o[i * WINDOW:(i + 1) * WINDOW].astype(BF16))

    def gate_a():
        ga_s[...] = _sigmoid(proj(OFF_GA, OFF_GB))

    def gate_b():
        gb_s[...] = _sigmoid(proj(OFF_GB, D_IN))

    tail_merge()
    hgrn_q()
    hgrn_f()
    hgrn_v()
    for c in range(n_chunks):
        hgrn_cumsum(c)
    att_q()
    att_kv()
    hgrn_prep(0)
    hgrn_prep(1)
    att_jobs = [(qb, kv) for qb in range(n_qb) for kv in range(ATT_KV_HEADS)]
    soft_jobs = list(att_jobs)
    val_jobs = list(att_jobs)
    early_fillers = [hgrn_gate, gate_a]
    late_fillers = [gate_b, tail_out]
    for c in range(n_chunks):
        hgrn_scores(c)
        hgrn_increment(c)
        if c >= 1:
            hgrn_intra(c - 1)
        if c + 2 < n_chunks:
            hgrn_prep(c + 2)
        if c % 2 == 0 and early_fillers:
            early_fillers.pop(0)()
        if c >= 1 and soft_jobs:
            att_softmax(*soft_jobs.pop(0))
    while early_fillers:
        early_fillers.pop(0)()
    hgrn_intra(n_chunks - 1)
    for c in range(n_chunks):
        hgrn_recur(c)
        if late_fillers:
            late_fillers.pop(0)()
        if soft_jobs:
            att_softmax(*soft_jobs.pop(0))
        if len(val_jobs) > len(soft_jobs) + 1 or (not soft_jobs and val_jobs):
            att_values(*val_jobs.pop(0))
    while late_fillers:
        late_fillers.pop(0)()
    while soft_jobs:
        att_softmax(*soft_jobs.pop(0))
    while val_jobs:
        att_values(*val_jobs.pop(0))
    for h in range(HGRN_HEADS):
        st_ref[h] = state[h]
    oa_bf[...] = (oa_s[...] * gate_s[...]).astype(BF16)
    kbuf[0:WINDOW, :] = kbuf[ts:ts + WINDOW, :]
    vbuf[0:WINDOW, :] = vbuf[ts:ts + WINDOW, :]


def _ffn_kernel(h_ref, g2_ref, win_ref, cw_ref, cb_ref, wd_ref, fg_ref, o_ref, gbuf, act_s, *, ts):
    t = pl.program_id(1)

    @pl.when(t == 0)
    def _():
        gbuf[0:SUBLANES, :] = jnp.zeros((SUBLANES, D_FF), F32)

    h = h_ref[0]
    u = _rmsnorm(h, g2_ref[...]).astype(BF16)

    def gate_up(lo, hi):
        g = jnp.dot(u, win_ref[:, lo:hi], preferred_element_type=F32)
        up = jnp.dot(u, win_ref[:, D_FF + lo:D_FF + hi], preferred_element_type=F32)
        gbuf[SUBLANES:SUBLANES + ts, lo:hi] = g
        a = (cw_ref[2:3, lo:hi] * g
             + cw_ref[1:2, lo:hi] * gbuf[SUBLANES - 1:SUBLANES - 1 + ts, lo:hi]
             + cw_ref[0:1, lo:hi] * gbuf[SUBLANES - 2:SUBLANES - 2 + ts, lo:hi]
             + cb_ref[:, lo:hi])
        act_s[:, lo:hi] = (_silu(a) * up).astype(BF16)

    def down(lo, hi):
        return jnp.dot(act_s[:, lo:hi], wd_ref[lo:hi, :], preferred_element_type=F32)

    y = None
    for j, (lo, hi) in enumerate(FFN_COL_CHUNKS):
        gate_up(lo, hi)
        if j >= 1:
            part = down(*FFN_COL_CHUNKS[j - 1])
            y = part if y is None else y + part
    y = y + down(*FFN_COL_CHUNKS[-1])
    gbuf[0:SUBLANES, :] = gbuf[ts:ts + SUBLANES, :]
    o_ref[0] = _rmsnorm(h + y, fg_ref[...])


def _const_spec(shape):
    return pl.BlockSpec(shape, lambda *_: (0,) * len(shape), pipeline_mode=pl.Buffered(1))


def _mixer_call(x, pos3, inv_tab, g1, lbl, hg, w_in, w_a, w_b, w_out, sinks):
    B, S, D = x.shape
    ts = MIX_TS
    assert S % ts == 0 and ts % WINDOW == 0 and ts % CHUNK == 0 and ts % LANES == 0
    tiles_per_seq = S // ts
    n_tiles = B * tiles_per_seq

    def head_tile(i):
        j = jnp.minimum(i, n_tiles - 1)
        return j // tiles_per_seq, j % tiles_per_seq

    def tail_tile(i):
        j = jnp.maximum(i - 1, 0)
        return j // tiles_per_seq, j % tiles_per_seq

    wide_f32 = pltpu.VMEM((ts, D_MODEL), F32)
    wide_bf16 = pltpu.VMEM((ts, D_MODEL), BF16)
    return pl.pallas_call(
        functools.partial(_mixer_kernel, ts=ts, n_tiles=n_tiles, tiles_per_seq=tiles_per_seq),
        grid=(n_tiles + 1,),
        in_specs=[
            pl.BlockSpec(memory_space=pltpu.SMEM),
            pl.BlockSpec((1, ts, D), lambda i: (*head_tile(i), 0)),
            pl.BlockSpec((1, ts, D), lambda i: (*tail_tile(i), 0)),
            pl.BlockSpec((1, 1, ts), lambda i: (head_tile(i)[0], 0, head_tile(i)[1])),
            _const_spec(inv_tab.shape),
            _const_spec(g1.shape),
            _const_spec(lbl.shape),
            _const_spec(hg.shape),
            _const_spec(w_in.shape),
            _const_spec(w_a.shape),
            _const_spec(w_b.shape),
            _const_spec(w_out.shape),
        ],
        out_specs=pl.BlockSpec((1, ts, D), lambda i: (*tail_tile(i), 0)),
        out_shape=jax.ShapeDtypeStruct((B, S, D), F32),
        scratch_shapes=[
            pltpu.VMEM((HGRN_HEADS, HGRN_DV, HGRN_DK), F32),
            pltpu.VMEM((WINDOW + ts, 4 * LANES), BF16),
            pltpu.VMEM((WINDOW + ts, 4 * LANES), BF16),
            wide_f32,
            wide_f32,
            pltpu.VMEM((2, ts, HGRN_F), BF16),
            wide_bf16,
            wide_f32,
            wide_bf16,
            wide_bf16,
            wide_bf16,
            wide_bf16,
            pltpu.VMEM((ts // CHUNK, HGRN_HEADS, HGRN_DV, HGRN_DK), F32),
            wide_f32,
            wide_f32,
            wide_f32,
            wide_f32,
            wide_bf16,
            pltpu.VMEM((ts // WINDOW * ATT_KV_HEADS, ATT_GROUP // 2 * WINDOW, 4 * WINDOW), BF16),
            wide_bf16,
            wide_bf16,
            wide_bf16,
        ],
        compiler_params=pltpu.CompilerParams(
            dimension_semantics=("arbitrary",), vmem_limit_bytes=VMEM_LIMIT_BYTES),
        name="token_mixer",
    )(sinks, x, x, pos3, inv_tab, g1, lbl, hg, w_in, w_a, w_b, w_out)


def _ffn_call(h, g2, w_ffn_in, conv_w, conv_b, w_down, fg):
    B, S, D = h.shape
    ts = FFN_TS
    assert S % ts == 0 and ts % SUBLANES == 0
    tile = pl.BlockSpec((1, ts, D), lambda b, t: (b, t, 0))
    return pl.pallas_call(
        functools.partial(_ffn_kernel, ts=ts),
        grid=(B, S // ts),
        in_specs=[
            tile,
            _const_spec(g2.shape),
            _const_spec(w_ffn_in.shape),
            _const_spec(conv_w.shape),
            _const_spec(conv_b.shape),
            _const_spec(w_down.shape),
            _const_spec(fg.shape),
        ],
        out_specs=tile,
        out_shape=jax.ShapeDtypeStruct((B, S, D), F32),
        scratch_shapes=[
            pltpu.VMEM((SUBLANES + ts, D_FF), F32),
            pltpu.VMEM((ts, D_FF), BF16),
        ],
        compiler_params=pltpu.CompilerParams(
            dimension_semantics=("parallel", "arbitrary"), vmem_limit_bytes=VMEM_LIMIT_BYTES),
        name="channel_mixer",
    )(h, g2, w_ffn_in, conv_w, conv_b, w_down, fg)


def kernel(x, positions, norm1_g, w_in, lb_logits, hgrn_norm_g, w_a, attn_sinks, w_b, w_out, norm2_g, w_ffn_in,
           conv_w, conv_b, w_down, final_g):
    B, S, D = x.shape
    assert norm1_g.shape[0] == 1, "single-layer stack"
    pos3 = positions.reshape(B, 1, S)
    inv = ROPE_THETA ** (-2.0 * jnp.arange(ROPE_HALF, dtype=F32) / ROPE_DIM)
    inv_tab = jnp.broadcast_to(inv[:, None], (ROPE_HALF, LANES))
    h = _mixer_call(
        x, pos3, inv_tab, norm1_g[0][None, :], lb_logits.astype(F32), hgrn_norm_g[0][None, :],
        w_in[0].astype(BF16), w_a[0].astype(BF16), w_b[0].astype(BF16), w_out[0].astype(BF16), attn_sinks[0])
    return _ffn_call(
        h, norm2_g[0][None, :], w_ffn_in[0].astype(BF16), conv_w[0], conv_b[0][None, :],
        w_down[0].astype(BF16), final_g[None, :])
```

```python
import functools

import jax
import jax.numpy as jnp
from jax import lax
from jax.experimental import pallas as pl
from jax.experimental.pallas import tpu as pltpu

D_MODEL = 1024
HGRN_HEADS = 8
HGRN_DK = 128
HGRN_DV = 128
HGRN_F = HGRN_HEADS * HGRN_DK
HGRN_V = HGRN_HEADS * HGRN_DV
CHUNK = 64
ATT_HEADS = 16
ATT_KV_HEADS = 2
ATT_HD = 64
ATT_GROUP = ATT_HEADS // ATT_KV_HEADS
WINDOW = 128
ROPE_DIM = ATT_HD // 4
ROPE_HALF = ROPE_DIM // 2
ROPE_THETA = 500000.0
D_FF = 2816
CONV_W = 3
EPS = 1e-6
NEG_INF = -1e30
LOG2_E = 1.4426950408889634

OFF_HQ = 0
OFF_HF = OFF_HQ + HGRN_F
OFF_HI = OFF_HF + HGRN_F
OFF_HG = OFF_HI + HGRN_V
OFF_AQ = OFF_HG + HGRN_V
OFF_AK = OFF_AQ + ATT_HEADS * ATT_HD
OFF_AV = OFF_AK + ATT_KV_HEADS * ATT_HD
OFF_GA = OFF_AV + ATT_KV_HEADS * ATT_HD
OFF_GB = OFF_GA + D_MODEL
D_IN = OFF_GB + D_MODEL

LANES = 128
SUBLANES = 8
VMEM_LIMIT_BYTES = 58 * 1024 * 1024

MIX_TS = 256
MIX_ORDER_TILE = (MIX_TS // CHUNK, MIX_TS // WINDOW)
MIX_ORDER = (
    "tm hq hf hv cs:0 cs:1 cs:2 cs:3 aq akv hgate pr:0 pr:1 "
    "sc:0 inc:0 pr:2 ga "
    "sc:1 inc:1 intra:0 pr:3 sm:0:0 "
    "sc:2 inc:2 intra:1 gb sm:0:1 "
    "sc:3 inc:3 intra:2 sm:1:0 intra:3 "
    "rec:0 to sm:1:1 av:0:0 rec:1 av:0:1 rec:2 av:1:0 rec:3 av:1:1")
FFN_TS = 512
FFN_COL_CHUNKS = ((0, 768), (768, 1536), (1536, 2304), (2304, 2816))

F32 = jnp.float32
BF16 = jnp.bfloat16
_NT = (((1,), (1,)), ((), ()))
_TN = (((0,), (0,)), ((), ()))


def _rmsnorm(x, g):
    return x * lax.rsqrt(jnp.mean(x * x, axis=-1, keepdims=True) + EPS) * g


def _sigmoid(x):
    return 0.5 * jnp.tanh(0.5 * x) + 0.5


def _silu(x):
    return x * _sigmoid(x)


def _mixer_kernel(sinks_ref, x_ref, xprev_ref, pos_ref, inv_ref, g1_ref, lbl_ref, hg_ref, win_ref, wa_ref,
                  wb_ref, wout_ref, h_ref, st_ref, kbuf, vbuf, qf_s, k_s, lf_s, v_s, b_s, qi_s, ki_s, ks_s, qo_s,
                  u_s, oa_s, gate_s, ga_s, gb_s, q_s, p_s, r_s, ob_s, oa_bf, mg_s, *, ts, n_tiles, tiles_per_seq):
    i = pl.program_id(0)
    t = jnp.minimum(i, n_tiles - 1) % tiles_per_seq
    n_chunks = ts // CHUNK
    n_qb = ts // WINDOW
    heads = [slice(h * HGRN_DK, (h + 1) * HGRN_DK) for h in range(HGRN_HEADS)]
    pairs = ATT_GROUP // 2

    @pl.when(i == 0)
    def _():
        oa_bf[...] = jnp.zeros_like(oa_bf)
        ob_s[...] = jnp.zeros_like(ob_s)
        ga_s[...] = jnp.zeros_like(ga_s)
        gb_s[...] = jnp.zeros_like(gb_s)

    @pl.when(t == 0)
    def _():
        st_ref[...] = jnp.zeros_like(st_ref)
        kbuf[0:WINDOW, :] = jnp.zeros((WINDOW, kbuf.shape[1]), BF16)
        vbuf[0:WINDOW, :] = jnp.zeros((WINDOW, vbuf.shape[1]), BF16)

    branch_out = {}

    def tail_ya():
        branch_out["a"] = jnp.dot(oa_bf[...], wa_ref[...], preferred_element_type=F32)

    def tail_yb():
        branch_out["b"] = jnp.dot(ob_s[...], wb_ref[...], preferred_element_type=F32)

    def tail_merge():
        mg_s[...] = (ga_s[...] * branch_out["a"] + gb_s[...] * branch_out["b"]).astype(BF16)

    def tail_out():
        h_ref[0] = xprev_ref[0] + jnp.dot(mg_s[...], wout_ref[...], preferred_element_type=F32)

    tail_ya()
    tail_yb()
    u = _rmsnorm(x_ref[0], g1_ref[...]).astype(BF16)

    def proj(lo, hi):
        return jnp.dot(u, win_ref[:, lo:hi], preferred_element_type=F32)

    def hgrn_q():
        qf_s[...] = _silu(proj(OFF_HQ, OFF_HF))

    def hgrn_f():
        lbl = lbl_ref[...]
        le = jnp.exp(lbl - jnp.max(lbl, axis=0, keepdims=True))
        lb = le[0:1] / (le[0:1] + le[1:2])
        f = lb + (1.0 - lb) * _sigmoid(proj(OFF_HF, OFF_HI))
        k_s[...] = 1.0 - f
        lf = jnp.log(f)
        lf_hi = lf.astype(BF16)
        lf_s[0] = lf_hi
        lf_s[1] = (lf - lf_hi.astype(F32)).astype(BF16)

    def hgrn_v():
        v_s[...] = proj(OFF_HI, OFF_HG).astype(BF16)

    def hgrn_gate():
        gate_s[...] = _silu(proj(OFF_HG, OFF_AQ))

    row = lax.broadcasted_iota(jnp.int32, (CHUNK, 2 * CHUNK), 0)
    col = lax.broadcasted_iota(jnp.int32, (CHUNK, 2 * CHUNK), 1)
    tri2 = (row >= (col % CHUNK)).astype(BF16)
    causal = (lax.broadcasted_iota(jnp.int32, (CHUNK, CHUNK), 0)
              >= lax.broadcasted_iota(jnp.int32, (CHUNK, CHUNK), 1))
    hg = hg_ref[...]

    def crows(c):
        return slice(c * CHUNK, (c + 1) * CHUNK)

    def hgrn_cumsum(c):
        rows = crows(c)
        l2 = jnp.concatenate([lf_s[0, rows, :], lf_s[1, rows, :]], axis=0)
        b_s[rows, :] = jnp.dot(tri2, l2, preferred_element_type=F32)

    dec = [None] * n_chunks

    def hgrn_prep(c):
        rows = crows(c)
        b = b_s[rows, :]
        bref = b[CHUNK // 2:CHUNK // 2 + 1]
        blast = b[CHUNK - 1:CHUNK]
        q_in = qf_s[rows, :] * jnp.exp(b - bref)
        k_in = k_s[rows, :] * jnp.exp(bref - b)
        qo_s[rows, :] = (q_in * jnp.exp(bref)).astype(BF16)
        ks_s[rows, :] = (k_in * jnp.exp(blast - bref)).astype(BF16)
        qi_s[rows, :] = q_in.astype(BF16)
        ki_s[rows, :] = k_in.astype(BF16)
        dec[c] = jnp.exp(blast)

    scores = [None] * n_chunks

    def hgrn_scores(c):
        rows = crows(c)
        scores[c] = [lax.dot_general(qi_s[rows, sl], ki_s[rows, sl], _NT, preferred_element_type=F32)
                     for sl in heads]

    def hgrn_increment(c):
        rows = crows(c)
        for h, sl in enumerate(heads):
            u_s[c, h] = lax.dot_general(v_s[rows, sl], ks_s[rows, sl], _TN, preferred_element_type=F32)

    def hgrn_intra(c):
        rows = crows(c)
        for h, sl in enumerate(heads):
            a = jnp.where(causal, scores[c][h], 0.0).astype(BF16)
            oa_s[rows, sl] = jnp.dot(a, v_s[rows, sl], preferred_element_type=F32)

    state = [None] * HGRN_HEADS

    def hgrn_recur(c):
        rows = crows(c)
        for h, sl in enumerate(heads):
            if c == 0:
                state[h] = st_ref[h]
            o = oa_s[rows, sl] + lax.dot_general(qo_s[rows, sl], state[h].astype(BF16), _NT,
                                                 preferred_element_type=F32)
            state[h] = state[h] * dec[c][:, sl] + u_s[c, h]
            oa_s[rows, sl] = _rmsnorm(o, hg)

    pos = pos_ref[0].astype(F32)
    inv = inv_ref[...]
    ang = jnp.concatenate([inv * pos[:, j * LANES:(j + 1) * LANES] for j in range(ts // LANES)], axis=1)
    c8 = jnp.cos(ang)
    s8 = jnp.sin(ang)
    one_pad = jnp.ones((ATT_HD - ROPE_DIM, ts), F32)
    zero_pad = jnp.zeros((ATT_HD - ROPE_DIM, ts), F32)
    z8 = jnp.zeros((ROPE_HALF, ts), F32)
    reps = LANES // ATT_HD
    cos_t = jnp.concatenate([c8, c8, one_pad] * reps, axis=0).T
    sin_lo_t = jnp.concatenate([z8, s8, zero_pad] * reps, axis=0).T
    sin_hi_t = jnp.concatenate([-s8, z8, zero_pad] * reps, axis=0).T

    def rope(xb):
        return (xb * cos_t + pltpu.roll(xb, ROPE_HALF, 1) * sin_lo_t
                + pltpu.roll(xb, LANES - ROPE_HALF, 1) * sin_hi_t)

    def att_q():
        zaq = proj(OFF_AQ, OFF_AK)
        scale = ATT_HD ** -0.5 * LOG2_E
        for i in range(ATT_HEADS * ATT_HD // LANES):
            sl = slice(i * LANES, (i + 1) * LANES)
            q_s[:, sl] = (rope(zaq[:, sl]) * scale).astype(BF16)

    def att_kv():
        zkv = proj(OFF_AK, OFF_GA)
        lo_half = lax.broadcasted_iota(jnp.int32, (ts, LANES), 1) < ATT_HD

        def spread(z):
            zr = pltpu.roll(z, ATT_HD, 1)
            return jnp.concatenate([jnp.where(lo_half, z, 0.0), jnp.where(lo_half, 0.0, zr),
                                    jnp.where(lo_half, zr, 0.0), jnp.where(lo_half, 0.0, z)],
                                   axis=1).astype(BF16)

        kbuf[WINDOW:WINDOW + ts, :] = spread(rope(zkv[:, 0:LANES]))
        vbuf[WINDOW:WINDOW + ts, :] = spread(zkv[:, LANES:2 * LANES])

    qi = lax.broadcasted_iota(jnp.int32, (WINDOW, 2 * WINDOW), 0)
    mi = lax.broadcasted_iota(jnp.int32, (WINDOW, 2 * WINDOW), 1)
    band = (mi > qi) & (mi <= qi + WINDOW)
    band_first = band & ((mi >= WINDOW) | (t > 0))
    lo_head = lax.broadcasted_iota(jnp.int32, (WINDOW, LANES), 1) < ATT_HD

    def band_rows(buf, qb, kv):
        krows = slice(qb * WINDOW, qb * WINDOW + 2 * WINDOW)
        c0 = kv * 2 * LANES
        return jnp.concatenate([buf[krows, c0:c0 + LANES], buf[krows, c0 + LANES:c0 + 2 * LANES]], axis=0)

    def att_softmax(qb, kv):
        qrows = slice(qb * WINDOW, (qb + 1) * WINDOW)
        mask = band_first if qb == 0 else band
        qs = jnp.concatenate(
            [q_s[qrows, (kv * pairs + i) * LANES:(kv * pairs + i + 1) * LANES] for i in range(pairs)], axis=0)
        s = lax.dot_general(qs, band_rows(kbuf, qb, kv), _NT, preferred_element_type=F32)
        for i in range(pairs):
            rden = []
            for e in range(2):
                blk = s[i * WINDOW:(i + 1) * WINDOW, e * 2 * WINDOW:(e + 1) * 2 * WINDOW]
                blk = jnp.where(mask, blk, NEG_INF)
                sink = sinks_ref[kv * ATT_GROUP + 2 * i + e] * LOG2_E
                m = jnp.maximum(jnp.max(blk, axis=-1, keepdims=True), sink)
                p = jnp.exp2(blk - m)
                p_s[qb * ATT_KV_HEADS + kv, i * WINDOW:(i + 1) * WINDOW,
                    e * 2 * WINDOW:(e + 1) * 2 * WINDOW] = p.astype(BF16)
                rden.append(1.0 / (jnp.sum(p, axis=-1, keepdims=True) + jnp.exp2(sink - m)))
            r_s[qb * ATT_KV_HEADS + kv, i] = jnp.where(lo_head, rden[0], rden[1])

    def att_values(qb, kv):
        qrows = slice(qb * WINDOW, (qb + 1) * WINDOW)
        o = jnp.dot(p_s[qb * ATT_KV_HEADS + kv], band_rows(vbuf, qb, kv), preferred_element_type=F32)
        for i in range(pairs):
            ob_s[qrows, (kv * pairs + i) * LANES:(kv * pairs + i + 1) * LANES] = (
                o[i * WINDOW:(i + 1) * WINDOW] * r_s[qb * ATT_KV_HEADS + kv, i]).astype(BF16)

    def gate_a():
        ga_s[...] = _sigmoid(proj(OFF_GA, OFF_GB))

    def gate_b():
        gb_s[...] = _sigmoid(proj(OFF_GB, D_IN))

    jobs = dict(tm=tail_merge, to=tail_out, hq=hgrn_q, hf=hgrn_f, hv=hgrn_v, cs=hgrn_cumsum, aq=att_q, akv=att_kv,
                pr=hgrn_prep, hgate=hgrn_gate, ga=gate_a, gb=gate_b, sc=hgrn_scores, inc=hgrn_increment,
                intra=hgrn_intra, rec=hgrn_recur, sm=att_softmax, av=att_values)
    assert (n_chunks, n_qb) == MIX_ORDER_TILE
    for item in MIX_ORDER.split():
        name, *args = item.split(":")
        jobs[name](*[int(a) for a in args])
    for h in range(HGRN_HEADS):
        st_ref[h] = state[h]
    oa_bf[...] = (oa_s[...] * gate_s[...]).astype(BF16)
    kbuf[0:WINDOW, :] = kbuf[ts:ts + WINDOW, :]
    vbuf[0:WINDOW, :] = vbuf[ts:ts + WINDOW, :]


def _ffn_kernel(h_ref, g2_ref, win_ref, cw_ref, cb_ref, wd_ref, fg_ref, o_ref, gbuf, act_s, *, ts):
    t = pl.program_id(1)

    @pl.when(t == 0)
    def _():
        gbuf[0:SUBLANES, :] = jnp.zeros((SUBLANES, D_FF), F32)

    h = h_ref[0]
    u = _rmsnorm(h, g2_ref[...]).astype(BF16)

    def gate_up(lo, hi):
        g = jnp.dot(u, win_ref[:, lo:hi], preferred_element_type=F32)
        up = jnp.dot(u, win_ref[:, D_FF + lo:D_FF + hi], preferred_element_type=F32)
        gbuf[SUBLANES:SUBLANES + ts, lo:hi] = g
        a = (cw_ref[2:3, lo:hi] * g
             + cw_ref[1:2, lo:hi] * gbuf[SUBLANES - 1:SUBLANES - 1 + ts, lo:hi]
             + cw_ref[0:1, lo:hi] * gbuf[SUBLANES - 2:SUBLANES - 2 + ts, lo:hi]
             + cb_ref[:, lo:hi])
        act_s[:, lo:hi] = (_silu(a) * up).astype(BF16)

    def down(lo, hi):
        return jnp.dot(act_s[:, lo:hi], wd_ref[lo:hi, :], preferred_element_type=F32)

    y = None
    for j, (lo, hi) in enumerate(FFN_COL_CHUNKS):
        gate_up(lo, hi)
        if j >= 1:
            part = down(*FFN_COL_CHUNKS[j - 1])
            y = part if y is None else y + part
    y = y + down(*FFN_COL_CHUNKS[-1])
    gbuf[0:SUBLANES, :] = gbuf[ts:ts + SUBLANES, :]
    o_ref[0] = _rmsnorm(h + y, fg_ref[...])


def _const_spec(shape):
    return pl.BlockSpec(shape, lambda *_: (0,) * len(shape), pipeline_mode=pl.Buffered(1))


def _mixer_call(x, pos3, inv_tab, g1, lbl, hg, w_in, w_a, w_b, w_out, sinks):
    B, S, D = x.shape
    ts = MIX_TS
    assert S % ts == 0 and ts % WINDOW == 0 and ts % CHUNK == 0 and ts % LANES == 0
    tiles_per_seq = S // ts
    n_tiles = B * tiles_per_seq

    def head_tile(i):
        j = jnp.minimum(i, n_tiles - 1)
        return j // tiles_per_seq, j % tiles_per_seq

    def tail_tile(i):
        j = jnp.maximum(i - 1, 0)
        return j // tiles_per_seq, j % tiles_per_seq

    wide_f32 = pltpu.VMEM((ts, D_MODEL), F32)
    wide_bf16 = pltpu.VMEM((ts, D_MODEL), BF16)
    return pl.pallas_call(
        functools.partial(_mixer_kernel, ts=ts, n_tiles=n_tiles, tiles_per_seq=tiles_per_seq),
        grid=(n_tiles + 1,),
        in_specs=[
            pl.BlockSpec(memory_space=pltpu.SMEM),
            pl.BlockSpec((1, ts, D), lambda i: (*head_tile(i), 0)),
            pl.BlockSpec((1, ts, D), lambda i: (*tail_tile(i), 0)),
            pl.BlockSpec((1, 1, ts), lambda i: (head_tile(i)[0], 0, head_tile(i)[1])),
            _const_spec(inv_tab.shape),
            _const_spec(g1.shape),
            _const_spec(lbl.shape),
            _const_spec(hg.shape),
            _const_spec(w_in.shape),
            _const_spec(w_a.shape),
            _const_spec(w_b.shape),
            _const_spec(w_out.shape),
        ],
        out_specs=pl.BlockSpec((1, ts, D), lambda i: (*tail_tile(i), 0)),
        out_shape=jax.ShapeDtypeStruct((B, S, D), F32),
        scratch_shapes=[
            pltpu.VMEM((HGRN_HEADS, HGRN_DV, HGRN_DK), F32),
            pltpu.VMEM((WINDOW + ts, 4 * LANES), BF16),
            pltpu.VMEM((WINDOW + ts, 4 * LANES), BF16),
            wide_f32,
            wide_f32,
            pltpu.VMEM((2, ts, HGRN_F), BF16),
            wide_bf16,
            wide_f32,
            wide_bf16,
            wide_bf16,
            wide_bf16,
            wide_bf16,
            pltpu.VMEM((ts // CHUNK, HGRN_HEADS, HGRN_DV, HGRN_DK), F32),
            wide_f32,
            wide_f32,
            wide_f32,
            wide_f32,
            wide_bf16,
            pltpu.VMEM((ts // WINDOW * ATT_KV_HEADS, ATT_GROUP // 2 * WINDOW, 4 * WINDOW), BF16),
            pltpu.VMEM((ts // WINDOW * ATT_KV_HEADS, ATT_GROUP // 2, WINDOW, LANES), F32),
            wide_bf16,
            wide_bf16,
            wide_bf16,
        ],
        compiler_params=pltpu.CompilerParams(
            dimension_semantics=("arbitrary",), vmem_limit_bytes=VMEM_LIMIT_BYTES),
        name="token_mixer",
    )(sinks, x, x, pos3, inv_tab, g1, lbl, hg, w_in, w_a, w_b, w_out)


def _ffn_call(h, g2, w_ffn_in, conv_w, conv_b, w_down, fg):
    B, S, D = h.shape
    ts = FFN_TS
    assert S % ts == 0 and ts % SUBLANES == 0
    tile = pl.BlockSpec((1, ts, D), lambda b, t: (b, t, 0))
    return pl.pallas_call(
        functools.partial(_ffn_kernel, ts=ts),
        grid=(B, S // ts),
        in_specs=[
            tile,
            _const_spec(g2.shape),
            _const_spec(w_ffn_in.shape),
            _const_spec(conv_w.shape),
            _const_spec(conv_b.shape),
            _const_spec(w_down.shape),
            _const_spec(fg.shape),
        ],
        out_specs=tile,
        out_shape=jax.ShapeDtypeStruct((B, S, D), F32),
        scratch_shapes=[
            pltpu.VMEM((SUBLANES + ts, D_FF), F32),
            pltpu.VMEM((ts, D_FF), BF16),
        ],
        compiler_params=pltpu.CompilerParams(
            dimension_semantics=("parallel", "arbitrary"), vmem_limit_bytes=VMEM_LIMIT_BYTES),
        name="channel_mixer",
    )(h, g2, w_ffn_in, conv_w, conv_b, w_down, fg)


def kernel(x, positions, norm1_g, w_in, lb_logits, hgrn_norm_g, w_a, attn_sinks, w_b, w_out, norm2_g, w_ffn_in,
           conv_w, conv_b, w_down, final_g):
    B, S, D = x.shape
    assert norm1_g.shape[0] == 1, "single-layer stack"
    pos3 = positions.reshape(B, 1, S)
    inv = ROPE_THETA ** (-2.0 * jnp.arange(ROPE_HALF, dtype=F32) / ROPE_DIM)
    inv_tab = jnp.broadcast_to(inv[:, None], (ROPE_HALF, LANES))
    h = _mixer_call(
        x, pos3, inv_tab, norm1_g[0][None, :], lb_logits.astype(F32), hgrn_norm_g[0][None, :],
        w_in[0].astype(BF16), w_a[0].astype(BF16), w_b[0].astype(BF16), w_out[0].astype(BF16), attn_sinks[0])
    return _ffn_call(
        h, norm2_g[0][None, :], w_ffn_in[0].astype(BF16), conv_w[0], conv_b[0][None, :],
        w_down[0].astype(BF16), final_g[None, :])
```

```python
import functools

import jax
import jax.numpy as jnp
from jax import lax
from jax.experimental import pallas as pl
from jax.experimental.pallas import tpu as pltpu

D_MODEL = 1024
HGRN_HEADS = 8
HGRN_DK = 128
HGRN_DV = 128
HGRN_F = HGRN_HEADS * HGRN_DK
HGRN_V = HGRN_HEADS * HGRN_DV
CHUNK = 64
ATT_HEADS = 16
ATT_KV_HEADS = 2
ATT_HD = 64
ATT_GROUP = ATT_HEADS // ATT_KV_HEADS
WINDOW = 128
ROPE_DIM = ATT_HD // 4
ROPE_HALF = ROPE_DIM // 2
ROPE_THETA = 500000.0
D_FF = 2816
CONV_W = 3
EPS = 1e-6
NEG_INF = -1e30
LOG2_E = 1.4426950408889634

OFF_HQ = 0
OFF_HF = OFF_HQ + HGRN_F
OFF_HI = OFF_HF + HGRN_F
OFF_HG = OFF_HI + HGRN_V
OFF_AQ = OFF_HG + HGRN_V
OFF_AK = OFF_AQ + ATT_HEADS * ATT_HD
OFF_AV = OFF_AK + ATT_KV_HEADS * ATT_HD
OFF_GA = OFF_AV + ATT_KV_HEADS * ATT_HD
OFF_GB = OFF_GA + D_MODEL
D_IN = OFF_GB + D_MODEL

LANES = 128
SUBLANES = 8
VMEM_LIMIT_BYTES = 58 * 1024 * 1024

MIX_TS = 256
MIX_ORDER_TILE = (MIX_TS // CHUNK, MIX_TS // WINDOW)
MIX_ORDER = (
    "tm hq hf hv cs:0 cs:1 cs:2 cs:3 aq akv hgate pr:0 pr:1 "
    "sc:0 inc:0 pr:2 ga "
    "sc:1 inc:1 intra:0 pr:3 sm:0:0 "
    "sc:2 inc:2 intra:1 gb sm:0:1 "
    "sc:3 inc:3 intra:2 sm:1:0 intra:3 "
    "rec:0 to sm:1:1 av:0:0 rec:1 av:0:1 rec:2 av:1:0 rec:3 av:1:1")
FFN_TS = 512
FFN_COL_CHUNKS = ((0, 768), (768, 1536), (1536, 2304), (2304, 2816))

F32 = jnp.float32
BF16 = jnp.bfloat16
_NT = (((1,), (1,)), ((), ()))
_TN = (((0,), (0,)), ((), ()))


def _rmsnorm(x, g):
    return x * lax.rsqrt(jnp.mean(x * x, axis=-1, keepdims=True) + EPS) * g


def _sigmoid(x):
    return 0.5 * jnp.tanh(0.5 * x) + 0.5


def _silu(x):
    return x * _sigmoid(x)


def _mixer_kernel(sinks_ref, x_ref, pos_ref, inv_ref, g1_ref, lbl_ref, hg_ref, win_ref, wa_ref,
                  wb_ref, wout_ref, h_ref, st_ref, kbuf, vbuf, qf_s, k_s, lf_s, v_s, b_s, qi_s, ki_s, ks_s, qo_s,
                  u_s, oa_s, gate_s, ga_s, gb_s, q_s, p_s, r_s, ob_s, oa_bf, mg_s, xprev_s, *, ts, n_tiles,
                  tiles_per_seq):
    i = pl.program_id(0)
    t = jnp.minimum(i, n_tiles - 1) % tiles_per_seq
    n_chunks = ts // CHUNK
    n_qb = ts // WINDOW
    heads = [slice(h * HGRN_DK, (h + 1) * HGRN_DK) for h in range(HGRN_HEADS)]
    pairs = ATT_GROUP // 2

    @pl.when(i == 0)
    def _():
        oa_bf[...] = jnp.zeros_like(oa_bf)
        ob_s[...] = jnp.zeros_like(ob_s)
        ga_s[...] = jnp.zeros_like(ga_s)
        gb_s[...] = jnp.zeros_like(gb_s)
        xprev_s[...] = jnp.zeros_like(xprev_s)

    @pl.when(t == 0)
    def _():
        st_ref[...] = jnp.zeros_like(st_ref)
        kbuf[0:WINDOW, :] = jnp.zeros((WINDOW, kbuf.shape[1]), BF16)
        vbuf[0:WINDOW, :] = jnp.zeros((WINDOW, vbuf.shape[1]), BF16)

    branch_out = {}

    def tail_ya():
        branch_out["a"] = jnp.dot(oa_bf[...], wa_ref[...], preferred_element_type=F32)

    def tail_yb():
        branch_out["b"] = jnp.dot(ob_s[...], wb_ref[...], preferred_element_type=F32)

    def tail_merge():
        mg_s[...] = (ga_s[...] * branch_out["a"] + gb_s[...] * branch_out["b"]).astype(BF16)

    def tail_out():
        h_ref[0] = xprev_s[...] + jnp.dot(mg_s[...], wout_ref[...], preferred_element_type=F32)

    tail_ya()
    tail_yb()
    u = _rmsnorm(x_ref[0], g1_ref[...]).astype(BF16)

    def proj(lo, hi):
        return jnp.dot(u, win_ref[:, lo:hi], preferred_element_type=F32)

    def hgrn_q():
        qf_s[...] = _silu(proj(OFF_HQ, OFF_HF))

    def hgrn_f():
        lbl = lbl_ref[...]
        le = jnp.exp(lbl - jnp.max(lbl, axis=0, keepdims=True))
        lb = le[0:1] / (le[0:1] + le[1:2])
        f = lb + (1.0 - lb) * _sigmoid(proj(OFF_HF, OFF_HI))
        k_s[...] = 1.0 - f
        lf = jnp.log(f)
        lf_hi = lf.astype(BF16)
        lf_s[0] = lf_hi
        lf_s[1] = (lf - lf_hi.astype(F32)).astype(BF16)

    def hgrn_v():
        v_s[...] = proj(OFF_HI, OFF_HG).astype(BF16)

    def hgrn_gate():
        gate_s[...] = _silu(proj(OFF_HG, OFF_AQ))

    row = lax.broadcasted_iota(jnp.int32, (CHUNK, 2 * CHUNK), 0)
    col = lax.broadcasted_iota(jnp.int32, (CHUNK, 2 * CHUNK), 1)
    tri2 = (row >= (col % CHUNK)).astype(BF16)
    causal = (lax.broadcasted_iota(jnp.int32, (CHUNK, CHUNK), 0)
              >= lax.broadcasted_iota(jnp.int32, (CHUNK, CHUNK), 1))
    hg = hg_ref[...]

    def crows(c):
        return slice(c * CHUNK, (c + 1) * CHUNK)

    def hgrn_cumsum(c):
        rows = crows(c)
        l2 = jnp.concatenate([lf_s[0, rows, :], lf_s[1, rows, :]], axis=0)
        b_s[rows, :] = jnp.dot(tri2, l2, preferred_element_type=F32)

    dec = [None] * n_chunks

    def hgrn_prep(c):
        rows = crows(c)
        b = b_s[rows, :]
        bref = b[CHUNK // 2:CHUNK // 2 + 1]
        blast = b[CHUNK - 1:CHUNK]
        q_in = qf_s[rows, :] * jnp.exp(b - bref)
        k_in = k_s[rows, :] * jnp.exp(bref - b)
        qo_s[rows, :] = (q_in * jnp.exp(bref)).astype(BF16)
        ks_s[rows, :] = (k_in * jnp.exp(blast - bref)).astype(BF16)
        qi_s[rows, :] = q_in.astype(BF16)
        ki_s[rows, :] = k_in.astype(BF16)
        dec[c] = jnp.exp(blast)

    scores = [None] * n_chunks

    def hgrn_scores(c):
        rows = crows(c)
        scores[c] = [lax.dot_general(qi_s[rows, sl], ki_s[rows, sl], _NT, preferred_element_type=F32)
                     for sl in heads]

    def hgrn_increment(c):
        rows = crows(c)
        for h, sl in enumerate(heads):
            u_s[c, h] = lax.dot_general(v_s[rows, sl], ks_s[rows, sl], _TN, preferred_element_type=F32)

    def hgrn_intra(c):
        rows = crows(c)
        for h, sl in enumerate(heads):
            a = jnp.where(causal, scores[c][h], 0.0).astype(BF16)
            oa_s[rows, sl] = jnp.dot(a, v_s[rows, sl], preferred_element_type=F32)

    state = [None] * HGRN_HEADS

    def hgrn_recur(c):
        rows = crows(c)
        for h, sl in enumerate(heads):
            if c == 0:
                state[h] = st_ref[h]
            o = oa_s[rows, sl] + lax.dot_general(qo_s[rows, sl], state[h].astype(BF16), _NT,
                                                 preferred_element_type=F32)
            state[h] = state[h] * dec[c][:, sl] + u_s[c, h]
            oa_s[rows, sl] = _rmsnorm(o, hg)

    pos = pos_ref[jnp.minimum(i, n_tiles - 1)].astype(F32)
    inv = inv_ref[...]
    ang = jnp.concatenate([inv * pos[:, j * LANES:(j + 1) * LANES] for j in range(ts // LANES)], axis=1)
    c8 = jnp.cos(ang)
    s8 = jnp.sin(ang)
    one_pad = jnp.ones((ATT_HD - ROPE_DIM, ts), F32)
    zero_pad = jnp.zeros((ATT_HD - ROPE_DIM, ts), F32)
    z8 = jnp.zeros((ROPE_HALF, ts), F32)
    reps = LANES // ATT_HD
    cos_t = jnp.concatenate([c8, c8, one_pad] * reps, axis=0).T
    sin_lo_t = jnp.concatenate([z8, s8, zero_pad] * reps, axis=0).T
    sin_hi_t = jnp.concatenate([-s8, z8, zero_pad] * reps, axis=0).T

    def rope(xb):
        return (xb * cos_t + pltpu.roll(xb, ROPE_HALF, 1) * sin_lo_t
                + pltpu.roll(xb, LANES - ROPE_HALF, 1) * sin_hi_t)

    def att_q():
        zaq = proj(OFF_AQ, OFF_AK)
        scale = ATT_HD ** -0.5 * LOG2_E
        for i in range(ATT_HEADS * ATT_HD // LANES):
            sl = slice(i * LANES, (i + 1) * LANES)
            q_s[:, sl] = (rope(zaq[:, sl]) * scale).astype(BF16)

    def att_kv():
        zkv = proj(OFF_AK, OFF_GA)
        lo_half = lax.broadcasted_iota(jnp.int32, (ts, LANES), 1) < ATT_HD

        def spread(z):
            zr = pltpu.roll(z, ATT_HD, 1)
            return jnp.concatenate([jnp.where(lo_half, z, 0.0), jnp.where(lo_half, 0.0, zr),
                                    jnp.where(lo_half, zr, 0.0), jnp.where(lo_half, 0.0, z)],
                                   axis=1).astype(BF16)

        kbuf[WINDOW:WINDOW + ts, :] = spread(rope(zkv[:, 0:LANES]))
        vbuf[WINDOW:WINDOW + ts, :] = spread(zkv[:, LANES:2 * LANES])

    qi = lax.broadcasted_iota(jnp.int32, (WINDOW, 2 * WINDOW), 0)
    mi = lax.broadcasted_iota(jnp.int32, (WINDOW, 2 * WINDOW), 1)
    band = (mi > qi) & (mi <= qi + WINDOW)
    band_first = band & ((mi >= WINDOW) | (t > 0))
    lo_head = lax.broadcasted_iota(jnp.int32, (WINDOW, LANES), 1) < ATT_HD

    def band_rows(buf, qb, kv):
        krows = slice(qb * WINDOW, qb * WINDOW + 2 * WINDOW)
        c0 = kv * 2 * LANES
        return jnp.concatenate([buf[krows, c0:c0 + LANES], buf[krows, c0 + LANES:c0 + 2 * LANES]], axis=0)

    def att_softmax(qb, kv):
        qrows = slice(qb * WINDOW, (qb + 1) * WINDOW)
        mask = band_first if qb == 0 else band
        qs = jnp.concatenate(
            [q_s[qrows, (kv * pairs + i) * LANES:(kv * pairs + i + 1) * LANES] for i in range(pairs)], axis=0)
        s = lax.dot_general(qs, band_rows(kbuf, qb, kv), _NT, preferred_element_type=F32)
        for i in range(pairs):
            rden = []
            for e in range(2):
                blk = s[i * WINDOW:(i + 1) * WINDOW, e * 2 * WINDOW:(e + 1) * 2 * WINDOW]
                blk = jnp.where(mask, blk, NEG_INF)
                sink = sinks_ref[kv * ATT_GROUP + 2 * i + e] * LOG2_E
                m = jnp.maximum(jnp.max(blk, axis=-1, keepdims=True), sink)
                p = jnp.exp2(blk - m)
                p_s[qb * ATT_KV_HEADS + kv, i * WINDOW:(i + 1) * WINDOW,
                    e * 2 * WINDOW:(e + 1) * 2 * WINDOW] = p.astype(BF16)
                rden.append(1.0 / (jnp.sum(p, axis=-1, keepdims=True) + jnp.exp2(sink - m)))
            r_s[qb * ATT_KV_HEADS + kv, i] = jnp.where(lo_head, rden[0], rden[1])

    def att_values(qb, kv):
        qrows = slice(qb * WINDOW, (qb + 1) * WINDOW)
        o = jnp.dot(p_s[qb * ATT_KV_HEADS + kv], band_rows(vbuf, qb, kv), preferred_element_type=F32)
        for i in range(pairs):
            ob_s[qrows, (kv * pairs + i) * LANES:(kv * pairs + i + 1) * LANES] = (
                o[i * WINDOW:(i + 1) * WINDOW] * r_s[qb * ATT_KV_HEADS + kv, i]).astype(BF16)

    def gate_a():
        ga_s[...] = _sigmoid(proj(OFF_GA, OFF_GB))

    def gate_b():
        gb_s[...] = _sigmoid(proj(OFF_GB, D_IN))

    jobs = dict(tm=tail_merge, to=tail_out, hq=hgrn_q, hf=hgrn_f, hv=hgrn_v, cs=hgrn_cumsum, aq=att_q, akv=att_kv,
                pr=hgrn_prep, hgate=hgrn_gate, ga=gate_a, gb=gate_b, sc=hgrn_scores, inc=hgrn_increment,
                intra=hgrn_intra, rec=hgrn_recur, sm=att_softmax, av=att_values)
    assert (n_chunks, n_qb) == MIX_ORDER_TILE
    for item in MIX_ORDER.split():
        name, *args = item.split(":")
        jobs[name](*[int(a) for a in args])
    for h in range(HGRN_HEADS):
        st_ref[h] = state[h]
    oa_bf[...] = (oa_s[...] * gate_s[...]).astype(BF16)
    xprev_s[...] = x_ref[0]
    kbuf[0:WINDOW, :] = kbuf[ts:ts + WINDOW, :]
    vbuf[0:WINDOW, :] = vbuf[ts:ts + WINDOW, :]


def _ffn_kernel(h_ref, g2_ref, win_ref, cw_ref, cb_ref, wd_ref, fg_ref, o_ref, gbuf, act_s, *, ts):
    t = pl.program_id(1)

    @pl.when(t == 0)
    def _():
        gbuf[0:SUBLANES, :] = jnp.zeros((SUBLANES, D_FF), F32)

    h = h_ref[0]
    u = _rmsnorm(h, g2_ref[...]).astype(BF16)

    def gate_up(lo, hi):
        g = jnp.dot(u, win_ref[:, lo:hi], preferred_element_type=F32)
        up = jnp.dot(u, win_ref[:, D_FF + lo:D_FF + hi], preferred_element_type=F32)
        gbuf[SUBLANES:SUBLANES + ts, lo:hi] = g
        a = (cw_ref[2:3, lo:hi] * g
             + cw_ref[1:2, lo:hi] * gbuf[SUBLANES - 1:SUBLANES - 1 + ts, lo:hi]
             + cw_ref[0:1, lo:hi] * gbuf[SUBLANES - 2:SUBLANES - 2 + ts, lo:hi]
             + cb_ref[:, lo:hi])
        act_s[:, lo:hi] = (_silu(a) * up).astype(BF16)

    def down(lo, hi):
        return jnp.dot(act_s[:, lo:hi], wd_ref[lo:hi, :], preferred_element_type=F32)

    y = None
    for j, (lo, hi) in enumerate(FFN_COL_CHUNKS):
        gate_up(lo, hi)
        if j >= 1:
            part = down(*FFN_COL_CHUNKS[j - 1])
            y = part if y is None else y + part
    y = y + down(*FFN_COL_CHUNKS[-1])
    gbuf[0:SUBLANES, :] = gbuf[ts:ts + SUBLANES, :]
    o_ref[0] = _rmsnorm(h + y, fg_ref[...])


def _const_spec(shape):
    return pl.BlockSpec(shape, lambda *_: (0,) * len(shape), pipeline_mode=pl.Buffered(1))


def _mixer_call(x, pos3, inv_tab, g1, lbl, hg, w_in, w_a, w_b, w_out, sinks):
    B, S, D = x.shape
    ts = MIX_TS
    assert S % ts == 0 and ts % WINDOW == 0 and ts % CHUNK == 0 and ts % LANES == 0
    tiles_per_seq = S // ts
    n_tiles = B * tiles_per_seq

    def head_tile(i):
        j = jnp.minimum(i, n_tiles - 1)
        return j // tiles_per_seq, j % tiles_per_seq

    def tail_tile(i):
        j = jnp.maximum(i - 1, 0)
        return j // tiles_per_seq, j % tiles_per_seq

    wide_f32 = pltpu.VMEM((ts, D_MODEL), F32)
    wide_bf16 = pltpu.VMEM((ts, D_MODEL), BF16)
    return pl.pallas_call(
        functools.partial(_mixer_kernel, ts=ts, n_tiles=n_tiles, tiles_per_seq=tiles_per_seq),
        grid=(n_tiles + 1,),
        in_specs=[
            pl.BlockSpec(memory_space=pltpu.SMEM),
            pl.BlockSpec((1, ts, D), lambda i: (*head_tile(i), 0)),
            _const_spec(pos3.shape),
            _const_spec(inv_tab.shape),
            _const_spec(g1.shape),
            _const_spec(lbl.shape),
            _const_spec(hg.shape),
            _const_spec(w_in.shape),
            _const_spec(w_a.shape),
            _const_spec(w_b.shape),
            _const_spec(w_out.shape),
        ],
        out_specs=pl.BlockSpec((1, ts, D), lambda i: (*tail_tile(i), 0)),
        out_shape=jax.ShapeDtypeStruct((B, S, D), F32),
        scratch_shapes=[
            pltpu.VMEM((HGRN_HEADS, HGRN_DV, HGRN_DK), F32),
            pltpu.VMEM((WINDOW + ts, 4 * LANES), BF16),
            pltpu.VMEM((WINDOW + ts, 4 * LANES), BF16),
            wide_f32,
            wide_f32,
            pltpu.VMEM((2, ts, HGRN_F), BF16),
            wide_bf16,
            wide_f32,
            wide_bf16,
            wide_bf16,
            wide_bf16,
            wide_bf16,
            pltpu.VMEM((ts // CHUNK, HGRN_HEADS, HGRN_DV, HGRN_DK), F32),
            wide_f32,
            wide_f32,
            wide_f32,
            wide_f32,
            wide_bf16,
            pltpu.VMEM((ts // WINDOW * ATT_KV_HEADS, ATT_GROUP // 2 * WINDOW, 4 * WINDOW), BF16),
            pltpu.VMEM((ts // WINDOW * ATT_KV_HEADS, ATT_GROUP // 2, WINDOW, LANES), F32),
            wide_bf16,
            wide_bf16,
            wide_bf16,
            wide_f32,
        ],
        compiler_params=pltpu.CompilerParams(
            dimension_semantics=("arbitrary",), vmem_limit_bytes=VMEM_LIMIT_BYTES),
        name="token_mixer",
    )(sinks, x, pos3, inv_tab, g1, lbl, hg, w_in, w_a, w_b, w_out)


def _ffn_call(h, g2, w_ffn_in, conv_w, conv_b, w_down, fg):
    B, S, D = h.shape
    ts = FFN_TS
    assert S % ts == 0 and ts % SUBLANES == 0
    tile = pl.BlockSpec((1, ts, D), lambda b, t: (b, t, 0))
    return pl.pallas_call(
        functools.partial(_ffn_kernel, ts=ts),
        grid=(B, S // ts),
        in_specs=[
            tile,
            _const_spec(g2.shape),
            _const_spec(w_ffn_in.shape),
            _const_spec(conv_w.shape),
            _const_spec(conv_b.shape),
            _const_spec(w_down.shape),
            _const_spec(fg.shape),
        ],
        out_specs=tile,
        out_shape=jax.ShapeDtypeStruct((B, S, D), F32),
        scratch_shapes=[
            pltpu.VMEM((SUBLANES + ts, D_FF), F32),
            pltpu.VMEM((ts, D_FF), BF16),
        ],
        compiler_params=pltpu.CompilerParams(
            dimension_semantics=("parallel", "arbitrary"), vmem_limit_bytes=VMEM_LIMIT_BYTES),
        name="channel_mixer",
    )(h, g2, w_ffn_in, conv_w, conv_b, w_down, fg)


def kernel(x, positions, norm1_g, w_in, lb_logits, hgrn_norm_g, w_a, attn_sinks, w_b, w_out, norm2_g, w_ffn_in,
           conv_w, conv_b, w_down, final_g):
    B, S, D = x.shape
    assert norm1_g.shape[0] == 1, "single-layer stack"
    pos3 = positions.reshape(B * (S // MIX_TS), 1, MIX_TS)
    inv = ROPE_THETA ** (-2.0 * jnp.arange(ROPE_HALF, dtype=F32) / ROPE_DIM)
    inv_tab = jnp.broadcast_to(inv[:, None], (ROPE_HALF, LANES))
    h = _mixer_call(
        x, pos3, inv_tab, norm1_g[0][None, :], lb_logits.astype(F32), hgrn_norm_g[0][None, :],
        w_in[0].astype(BF16), w_a[0].astype(BF16), w_b[0].astype(BF16), w_out[0].astype(BF16), attn_sinks[0])
    return _ffn_call(
        h, norm2_g[0][None, :], w_ffn_in[0].astype(BF16), conv_w[0], conv_b[0][None, :],
        w_down[0].astype(BF16), final_g[None, :])
```

```python
import functools

import jax
import jax.numpy as jnp
from jax import lax
from jax.experimental import pallas as pl
from jax.experimental.pallas import tpu as pltpu

D_MODEL = 1024
HGRN_HEADS = 8
HGRN_DK = 128
HGRN_DV = 128
HGRN_F = HGRN_HEADS * HGRN_DK
HGRN_V = HGRN_HEADS * HGRN_DV
CHUNK = 64
ATT_HEADS = 16
ATT_KV_HEADS = 2
ATT_HD = 64
ATT_GROUP = ATT_HEADS // ATT_KV_HEADS
WINDOW = 128
ROPE_DIM = ATT_HD // 4
ROPE_HALF = ROPE_DIM // 2
ROPE_THETA = 500000.0
D_FF = 2816
CONV_W = 3
EPS = 1e-6
NEG_INF = -1e30
LOG2_E = 1.4426950408889634

OFF_HQ = 0
OFF_HF = OFF_HQ + HGRN_F
OFF_HI = OFF_HF + HGRN_F
OFF_HG = OFF_HI + HGRN_V
OFF_AQ = OFF_HG + HGRN_V
OFF_AK = OFF_AQ + ATT_HEADS * ATT_HD
OFF_AV = OFF_AK + ATT_KV_HEADS * ATT_HD
OFF_GA = OFF_AV + ATT_KV_HEADS * ATT_HD
OFF_GB = OFF_GA + D_MODEL
D_IN = OFF_GB + D_MODEL

LANES = 128
SUBLANES = 8
VMEM_LIMIT_BYTES = 58 * 1024 * 1024

MIX_TS = 256
MIX_ORDER_TILE = (MIX_TS // CHUNK, MIX_TS // WINDOW)
MIX_ORDER = (
    "tm hq hf hv cs:0 cs:1 cs:2 cs:3 aq akv hgate pr:0 pr:1 "
    "sc:0 inc:0 pr:2 ga "
    "sc:1 inc:1 intra:0 pr:3 sm:0:0 "
    "sc:2 inc:2 intra:1 gb sm:0:1 "
    "sc:3 inc:3 intra:2 sm:1:0 intra:3 "
    "rec:0 to sm:1:1 av:0:0 rec:1 av:0:1 rec:2 av:1:0 rec:3 av:1:1")
FFN_TS = 512
FFN_COL_CHUNKS = ((0, 768), (768, 1536), (1536, 2304), (2304, 2816))

F32 = jnp.float32
BF16 = jnp.bfloat16
_NT = (((1,), (1,)), ((), ()))
_TN = (((0,), (0,)), ((), ()))


def _rmsnorm(x, g):
    return x * lax.rsqrt(jnp.mean(x * x, axis=-1, keepdims=True) + EPS) * g


def _sigmoid(x):
    return 0.5 * jnp.tanh(0.5 * x) + 0.5


def _silu(x):
    return x * _sigmoid(x)


def _mixer_kernel(sinks_ref, x_ref, xprev_ref, pos_ref, inv_ref, g1_ref, lbl_ref, hg_ref, win_ref, wa_ref,
                  wb_ref, wout_ref, h_ref, st_ref, kbuf, vbuf, qf_s, k_s, lf_s, v_s, b_s, qi_s, ki_s, ks_s, qo_s,
                  u_s, oa_s, gate_s, ga_s, gb_s, q_s, p_s, r_s, ob_s, oa_bf, mg_s, *, ts, n_tiles, tiles_per_seq):
    i = pl.program_id(0)
    t = jnp.minimum(i, n_tiles - 1) % tiles_per_seq
    n_chunks = ts // CHUNK
    n_qb = ts // WINDOW
    heads = [slice(h * HGRN_DK, (h + 1) * HGRN_DK) for h in range(HGRN_HEADS)]
    pairs = ATT_GROUP // 2

    @pl.when(i == 0)
    def _():
        oa_bf[...] = jnp.zeros_like(oa_bf)
        ob_s[...] = jnp.zeros_like(ob_s)
        ga_s[...] = jnp.zeros_like(ga_s)
        gb_s[...] = jnp.zeros_like(gb_s)

    @pl.when(t == 0)
    def _():
        st_ref[...] = jnp.zeros_like(st_ref)
        kbuf[0:WINDOW, :] = jnp.zeros((WINDOW, kbuf.shape[1]), BF16)
        vbuf[0:WINDOW, :] = jnp.zeros((WINDOW, vbuf.shape[1]), BF16)

    branch_out = {}

    def tail_ya():
        branch_out["a"] = jnp.dot(oa_bf[...], wa_ref[...], preferred_element_type=F32)

    def tail_yb():
        branch_out["b"] = jnp.dot(ob_s[...], wb_ref[...], preferred_element_type=F32)

    def tail_merge():
        mg_s[...] = (ga_s[...] * branch_out["a"] + gb_s[...] * branch_out["b"]).astype(BF16)

    def tail_out():
        h_ref[0] = xprev_ref[0] + jnp.dot(mg_s[...], wout_ref[...], preferred_element_type=F32)

    tail_ya()
    tail_yb()
    u = _rmsnorm(x_ref[0], g1_ref[...]).astype(BF16)

    def proj(lo, hi):
        return jnp.dot(u, win_ref[:, lo:hi], preferred_element_type=F32)

    def hgrn_q():
        qf_s[...] = _silu(proj(OFF_HQ, OFF_HF))

    def hgrn_f():
        lbl = lbl_ref[...]
        le = jnp.exp(lbl - jnp.max(lbl, axis=0, keepdims=True))
        lb = le[0:1] / (le[0:1] + le[1:2])
        f = lb + (1.0 - lb) * _sigmoid(proj(OFF_HF, OFF_HI))
        k_s[...] = 1.0 - f
        lf = jnp.log(f)
        lf_hi = lf.astype(BF16)
        lf_s[0] = lf_hi
        lf_s[1] = (lf - lf_hi.astype(F32)).astype(BF16)

    def hgrn_v():
        v_s[...] = proj(OFF_HI, OFF_HG).astype(BF16)

    def hgrn_gate():
        gate_s[...] = _silu(proj(OFF_HG, OFF_AQ))

    row = lax.broadcasted_iota(jnp.int32, (CHUNK, 2 * CHUNK), 0)
    col = lax.broadcasted_iota(jnp.int32, (CHUNK, 2 * CHUNK), 1)
    tri2 = (row >= (col % CHUNK)).astype(BF16)
    causal = (lax.broadcasted_iota(jnp.int32, (CHUNK, CHUNK), 0)
              >= lax.broadcasted_iota(jnp.int32, (CHUNK, CHUNK), 1))
    hg = hg_ref[...]

    def crows(c):
        return slice(c * CHUNK, (c + 1) * CHUNK)

    def hgrn_cumsum(c):
        rows = crows(c)
        l2 = jnp.concatenate([lf_s[0, rows, :], lf_s[1, rows, :]], axis=0)
        b_s[rows, :] = jnp.dot(tri2, l2, preferred_element_type=F32)

    dec = [None] * n_chunks

    def hgrn_prep(c):
        rows = crows(c)
        b = b_s[rows, :]
        bref = b[CHUNK // 2:CHUNK // 2 + 1]
        blast = b[CHUNK - 1:CHUNK]
        q_in = qf_s[rows, :] * jnp.exp(b - bref)
        k_in = k_s[rows, :] * jnp.exp(bref - b)
        qo_s[rows, :] = (q_in * jnp.exp(bref)).astype(BF16)
        ks_s[rows, :] = (k_in * jnp.exp(blast - bref)).astype(BF16)
        qi_s[rows, :] = q_in.astype(BF16)
        ki_s[rows, :] = k_in.astype(BF16)
        dec[c] = jnp.broadcast_to(jnp.exp(blast), (SUBLANES, HGRN_F)).T[:, 0:1]

    scores = [None] * n_chunks

    def hgrn_scores(c):
        rows = crows(c)
        scores[c] = [lax.dot_general(qi_s[rows, sl], ki_s[rows, sl], _NT, preferred_element_type=F32)
                     for sl in heads]

    def hgrn_increment(c):
        rows = crows(c)
        for h, sl in enumerate(heads):
            u_s[c, h] = lax.dot_general(ks_s[rows, sl], v_s[rows, sl], _TN, preferred_element_type=F32)

    def hgrn_intra(c):
        rows = crows(c)
        for h, sl in enumerate(heads):
            a = jnp.where(causal, scores[c][h], 0.0).astype(BF16)
            oa_s[rows, sl] = jnp.dot(a, v_s[rows, sl], preferred_element_type=F32)

    state = [None] * HGRN_HEADS

    def hgrn_recur(c):
        rows = crows(c)
        for h, sl in enumerate(heads):
            if c == 0:
                state[h] = st_ref[h]
            o = oa_s[rows, sl] + jnp.dot(qo_s[rows, sl], state[h].astype(BF16), preferred_element_type=F32)
            state[h] = state[h] * dec[c][sl, :] + u_s[c, h]
            oa_s[rows, sl] = _rmsnorm(o, hg)

    pos = pos_ref[0].astype(F32)
    inv = inv_ref[...]
    ang = jnp.concatenate([inv * pos[:, j * LANES:(j + 1) * LANES] for j in range(ts // LANES)], axis=1)
    c8 = jnp.cos(ang)
    s8 = jnp.sin(ang)
    one_pad = jnp.ones((ATT_HD - ROPE_DIM, ts), F32)
    zero_pad = jnp.zeros((ATT_HD - ROPE_DIM, ts), F32)
    z8 = jnp.zeros((ROPE_HALF, ts), F32)
    reps = LANES // ATT_HD
    cos_t = jnp.concatenate([c8, c8, one_pad] * reps, axis=0).T
    sin_lo_t = jnp.concatenate([z8, s8, zero_pad] * reps, axis=0).T
    sin_hi_t = jnp.concatenate([-s8, z8, zero_pad] * reps, axis=0).T

    def rope(xb):
        return (xb * cos_t + pltpu.roll(xb, ROPE_HALF, 1) * sin_lo_t
                + pltpu.roll(xb, LANES - ROPE_HALF, 1) * sin_hi_t)

    def att_q():
        zaq = proj(OFF_AQ, OFF_AK)
        scale = ATT_HD ** -0.5 * LOG2_E
        for i in range(ATT_HEADS * ATT_HD // LANES):
            sl = slice(i * LANES, (i + 1) * LANES)
            q_s[:, sl] = (rope(zaq[:, sl]) * scale).astype(BF16)

    def att_kv():
        zkv = proj(OFF_AK, OFF_GA)
        lo_half = lax.broadcasted_iota(jnp.int32, (ts, LANES), 1) < ATT_HD

        def spread(z):
            zr = pltpu.roll(z, ATT_HD, 1)
            return jnp.concatenate([jnp.where(lo_half, z, 0.0), jnp.where(lo_half, 0.0, zr),
                                    jnp.where(lo_half, zr, 0.0), jnp.where(lo_half, 0.0, z)],
                                   axis=1).astype(BF16)

        kbuf[WINDOW:WINDOW + ts, :] = spread(rope(zkv[:, 0:LANES]))
        vbuf[WINDOW:WINDOW + ts, :] = spread(zkv[:, LANES:2 * LANES])

    qi = lax.broadcasted_iota(jnp.int32, (WINDOW, 2 * WINDOW), 0)
    mi = lax.broadcasted_iota(jnp.int32, (WINDOW, 2 * WINDOW), 1)
    band = (mi > qi) & (mi <= qi + WINDOW)
    band_first = band & ((mi >= WINDOW) | (t > 0))
    lo_head = lax.broadcasted_iota(jnp.int32, (WINDOW, LANES), 1) < ATT_HD

    def band_rows(buf, qb, kv):
        krows = slice(qb * WINDOW, qb * WINDOW + 2 * WINDOW)
        c0 = kv * 2 * LANES
        return jnp.concatenate([buf[krows, c0:c0 + LANES], buf[krows, c0 + LANES:c0 + 2 * LANES]], axis=0)

    def att_softmax(qb, kv):
        qrows = slice(qb * WINDOW, (qb + 1) * WINDOW)
        mask = band_first if qb == 0 else band
        qs = jnp.concatenate(
            [q_s[qrows, (kv * pairs + i) * LANES:(kv * pairs + i + 1) * LANES] for i in range(pairs)], axis=0)
        s = lax.dot_general(qs, band_rows(kbuf, qb, kv), _NT, preferred_element_type=F32)
        for i in range(pairs):
            rden = []
            for e in range(2):
                blk = s[i * WINDOW:(i + 1) * WINDOW, e * 2 * WINDOW:(e + 1) * 2 * WINDOW]
                blk = jnp.where(mask, blk, NEG_INF)
                sink = sinks_ref[kv * ATT_GROUP + 2 * i + e] * LOG2_E
                m = jnp.maximum(jnp.max(blk, axis=-1, keepdims=True), sink)
                p = jnp.exp2(blk - m)
                p_s[qb * ATT_KV_HEADS + kv, i * WINDOW:(i + 1) * WINDOW,
                    e * 2 * WINDOW:(e + 1) * 2 * WINDOW] = p.astype(BF16)
                rden.append(1.0 / (jnp.sum(p, axis=-1, keepdims=True) + jnp.exp2(sink - m)))
            r_s[qb * ATT_KV_HEADS + kv, i] = jnp.where(lo_head, rden[0], rden[1])

    def att_values(qb, kv):
        qrows = slice(qb * WINDOW, (qb + 1) * WINDOW)
        o = jnp.dot(p_s[qb * ATT_KV_HEADS + kv], band_rows(vbuf, qb, kv), preferred_element_type=F32)
        for i in range(pairs):
            ob_s[qrows, (kv * pairs + i) * LANES:(kv * pairs + i + 1) * LANES] = (
                o[i * WINDOW:(i + 1) * WINDOW] * r_s[qb * ATT_KV_HEADS + kv, i]).astype(BF16)

    def gate_a():
        ga_s[...] = _sigmoid(proj(OFF_GA, OFF_GB))

    def gate_b():
        gb_s[...] = _sigmoid(proj(OFF_GB, D_IN))

    jobs = dict(tm=tail_merge, to=tail_out, hq=hgrn_q, hf=hgrn_f, hv=hgrn_v, cs=hgrn_cumsum, aq=att_q, akv=att_kv,
                pr=hgrn_prep, hgate=hgrn_gate, ga=gate_a, gb=gate_b, sc=hgrn_scores, inc=hgrn_increment,
                intra=hgrn_intra, rec=hgrn_recur, sm=att_softmax, av=att_values)
    assert (n_chunks, n_qb) == MIX_ORDER_TILE
    for item in MIX_ORDER.split():
        name, *args = item.split(":")
        jobs[name](*[int(a) for a in args])
    for h in range(HGRN_HEADS):
        st_ref[h] = state[h]
    oa_bf[...] = (oa_s[...] * gate_s[...]).astype(BF16)
    kbuf[0:WINDOW, :] = kbuf[ts:ts + WINDOW, :]
    vbuf[0:WINDOW, :] = vbuf[ts:ts + WINDOW, :]


def _ffn_kernel(h_ref, g2_ref, win_ref, cw_ref, cb_ref, wd_ref, fg_ref, o_ref, gbuf, act_s, *, ts):
    t = pl.program_id(1)

    @pl.when(t == 0)
    def _():
        gbuf[0:SUBLANES, :] = jnp.zeros((SUBLANES, D_FF), F32)

    h = h_ref[0]
    u = _rmsnorm(h, g2_ref[...]).astype(BF16)

    def gate_up(lo, hi):
        g = jnp.dot(u, win_ref[:, lo:hi], preferred_element_type=F32)
        up = jnp.dot(u, win_ref[:, D_FF + lo:D_FF + hi], preferred_element_type=F32)
        gbuf[SUBLANES:SUBLANES + ts, lo:hi] = g
        a = (cw_ref[2:3, lo:hi] * g
             + cw_ref[1:2, lo:hi] * gbuf[SUBLANES - 1:SUBLANES - 1 + ts, lo:hi]
             + cw_ref[0:1, lo:hi] * gbuf[SUBLANES - 2:SUBLANES - 2 + ts, lo:hi]
             + cb_ref[:, lo:hi])
        act_s[:, lo:hi] = (_silu(a) * up).astype(BF16)

    def down(lo, hi):
        return jnp.dot(act_s[:, lo:hi], wd_ref[lo:hi, :], preferred_element_type=F32)

    y = None
    for j, (lo, hi) in enumerate(FFN_COL_CHUNKS):
        gate_up(lo, hi)
        if j >= 1:
            part = down(*FFN_COL_CHUNKS[j - 1])
            y = part if y is None else y + part
    y = y + down(*FFN_COL_CHUNKS[-1])
    gbuf[0:SUBLANES, :] = gbuf[ts:ts + SUBLANES, :]
    o_ref[0] = _rmsnorm(h + y, fg_ref[...])


def _const_spec(shape):
    return pl.BlockSpec(shape, lambda *_: (0,) * len(shape), pipeline_mode=pl.Buffered(1))


def _mixer_call(x, pos3, inv_tab, g1, lbl, hg, w_in, w_a, w_b, w_out, sinks):
    B, S, D = x.shape
    ts = MIX_TS
    assert S % ts == 0 and ts % WINDOW == 0 and ts % CHUNK == 0 and ts % LANES == 0
    tiles_per_seq = S // ts
    n_tiles = B * tiles_per_seq

    def head_tile(i):
        j = jnp.minimum(i, n_tiles - 1)
        return j // tiles_per_seq, j % tiles_per_seq

    def tail_tile(i):
        j = jnp.maximum(i - 1, 0)
        return j // tiles_per_seq, j % tiles_per_seq

    wide_f32 = pltpu.VMEM((ts, D_MODEL), F32)
    wide_bf16 = pltpu.VMEM((ts, D_MODEL), BF16)
    return pl.pallas_call(
        functools.partial(_mixer_kernel, ts=ts, n_tiles=n_tiles, tiles_per_seq=tiles_per_seq),
        grid=(n_tiles + 1,),
        in_specs=[
            pl.BlockSpec(memory_space=pltpu.SMEM),
            pl.BlockSpec((1, ts, D), lambda i: (*head_tile(i), 0)),
            pl.BlockSpec((1, ts, D), lambda i: (*tail_tile(i), 0)),
            pl.BlockSpec((1, 1, ts), lambda i: (head_tile(i)[0], 0, head_tile(i)[1])),
            _const_spec(inv_tab.shape),
            _const_spec(g1.shape),
            _const_spec(lbl.shape),
            _const_spec(hg.shape),
            _const_spec(w_in.shape),
            _const_spec(w_a.shape),
            _const_spec(w_b.shape),
            _const_spec(w_out.shape),
        ],
        out_specs=pl.BlockSpec((1, ts, D), lambda i: (*tail_tile(i), 0)),
        out_shape=jax.ShapeDtypeStruct((B, S, D), F32),
        scratch_shapes=[
            pltpu.VMEM((HGRN_HEADS, HGRN_DK, HGRN_DV), F32),
            pltpu.VMEM((WINDOW + ts, 4 * LANES), BF16),
            pltpu.VMEM((WINDOW + ts, 4 * LANES), BF16),
            wide_f32,
            wide_f32,
            pltpu.VMEM((2, ts, HGRN_F), BF16),
            wide_bf16,
            wide_f32,
            wide_bf16,
            wide_bf16,
            wide_bf16,
            wide_bf16,
            pltpu.VMEM((ts // CHUNK, HGRN_HEADS, HGRN_DK, HGRN_DV), F32),
            wide_f32,
            wide_f32,
            wide_f32,
            wide_f32,
            wide_bf16,
            pltpu.VMEM((ts // WINDOW * ATT_KV_HEADS, ATT_GROUP // 2 * WINDOW, 4 * WINDOW), BF16),
            pltpu.VMEM((ts // WINDOW * ATT_KV_HEADS, ATT_GROUP // 2, WINDOW, LANES), F32),
            wide_bf16,
            wide_bf16,
            wide_bf16,
        ],
        compiler_params=pltpu.CompilerParams(
            dimension_semantics=("arbitrary",), vmem_limit_bytes=VMEM_LIMIT_BYTES),
        name="token_mixer",
    )(sinks, x, x, pos3, inv_tab, g1, lbl, hg, w_in, w_a, w_b, w_out)


def _ffn_call(h, g2, w_ffn_in, conv_w, conv_b, w_down, fg):
    B, S, D = h.shape
    ts = FFN_TS
    assert S % ts == 0 and ts % SUBLANES == 0
    tile = pl.BlockSpec((1, ts, D), lambda b, t: (b, t, 0))
    return pl.pallas_call(
        functools.partial(_ffn_kernel, ts=ts),
        grid=(B, S // ts),
        in_specs=[
            tile,
            _const_spec(g2.shape),
            _const_spec(w_ffn_in.shape),
            _const_spec(conv_w.shape),
            _const_spec(conv_b.shape),
            _const_spec(w_down.shape),
            _const_spec(fg.shape),
        ],
        out_specs=tile,
        out_shape=jax.ShapeDtypeStruct((B, S, D), F32),
        scratch_shapes=[
            pltpu.VMEM((SUBLANES + ts, D_FF), F32),
            pltpu.VMEM((ts, D_FF), BF16),
        ],
        compiler_params=pltpu.CompilerParams(
            dimension_semantics=("parallel", "arbitrary"), vmem_limit_bytes=VMEM_LIMIT_BYTES),
        name="channel_mixer",
    )(h, g2, w_ffn_in, conv_w, conv_b, w_down, fg)


def kernel(x, positions, norm1_g, w_in, lb_logits, hgrn_norm_g, w_a, attn_sinks, w_b, w_out, norm2_g, w_ffn_in,
           conv_w, conv_b, w_down, final_g):
    B, S, D = x.shape
    assert norm1_g.shape[0] == 1, "single-layer stack"
    pos3 = positions.reshape(B, 1, S)
    inv = ROPE_THETA ** (-2.0 * jnp.arange(ROPE_HALF, dtype=F32) / ROPE_DIM)
    inv_tab = jnp.broadcast_to(inv[:, None], (ROPE_HALF, LANES))
    h = _mixer_call(
        x, pos3, inv_tab, norm1_g[0][None, :], lb_logits.astype(F32), hgrn_norm_g[0][None, :],
        w_in[0].astype(BF16), w_a[0].astype(BF16), w_b[0].astype(BF16), w_out[0].astype(BF16), attn_sinks[0])
    return _ffn_call(
        h, norm2_g[0][None, :], w_ffn_in[0].astype(BF16), conv_w[0], conv_b[0][None, :],
        w_down[0].astype(BF16), final_g[None, :])
```
